```python
import jax, jax.numpy as jnp
from jax import lax
import numpy as np

D_MODEL = 1024
BATCH = 4
SEQ = 4096
DEPTH = 4
DEC_BATCH = 2
DEC_SEQ = 16384
PAST_LEN = 128

HEAD_DIM = 64
ATTN_HEADS = 8
KV_HEADS = 2
Q_PER_KV = ATTN_HEADS // KV_HEADS
WINDOW = 128
BLOCK = 128
ATTN_W = ATTN_HEADS * HEAD_DIM
KV_W = KV_HEADS * HEAD_DIM
SGU_GROUPS = 8
SGU_DIM = 64
SGU_W = SGU_GROUPS * SGU_DIM
CHUNK = 128
CONV_CH = 512
CONV_WIDTH = 31
POOL_GROUPS = 4
POOL_DIM = 128
POOL_W = POOL_GROUPS * POOL_DIM
POOL_WINDOWS = (2, 4, 8, 16)
IN_AB = ATTN_W + 2 * KV_W + 2 * SGU_W
MIX_AB = ATTN_W + SGU_W
IN_CD = 2 * CONV_CH + POOL_W
MIX_CD = CONV_CH + POOL_W
N_EXPERTS = 16
EXPERT_FF = 1536
CAPACITY_FACTOR = 2
EPS = 1e-6
NEG_INF = -1e30
N_EVEN = (DEPTH + 1) // 2
N_ODD = DEPTH // 2

kernel_name = "hybrid_bidir_encoder_ec_moe"


def rms_norm(x, g):
    xf = x.astype(jnp.float32)
    y = xf * lax.rsqrt(jnp.mean(xf * xf, axis=-1, keepdims=True) + EPS)
    return (y * g.astype(jnp.float32)).astype(x.dtype)


def alibi_slopes():
    return jnp.exp2(-8.0 * jnp.arange(1, ATTN_HEADS + 1, dtype=jnp.float32) / ATTN_HEADS)


def windowed_gqa(q, k, v, q_gain, k_gain, sink):
    B, S, _ = q.shape
    nb = S // BLOCK
    q = rms_norm(q.reshape(B, S, ATTN_HEADS, HEAD_DIM), q_gain)
    k = rms_norm(k.reshape(B, S, KV_HEADS, HEAD_DIM), k_gain)
    v = v.reshape(B, S, KV_HEADS, HEAD_DIM)
    qb = q.reshape(B, nb, BLOCK, KV_HEADS, Q_PER_KV, HEAD_DIM)

    def band(t):
        tp = jnp.pad(t, ((0, 0), (BLOCK, BLOCK), (0, 0), (0, 0)))
        tp = tp.reshape(B, nb + 2, BLOCK, KV_HEADS, HEAD_DIM)
        return jnp.concatenate([tp[:, :-2], tp[:, 1:-1], tp[:, 2:]], axis=2)

    kw, vw = band(k), band(v)
    s = jnp.einsum('bnqkgd,bnskd->bnkgqs', qb, kw).astype(jnp.float32) * (HEAD_DIM ** -0.5)
    rel = jnp.arange(3 * BLOCK)[None, :] - BLOCK - jnp.arange(BLOCK)[:, None]
    key_pos = (jnp.arange(nb)[:, None] - 1) * BLOCK + jnp.arange(3 * BLOCK)[None, :]
    mask = (jnp.abs(rel) <= WINDOW)[None] & ((key_pos >= 0) & (key_pos < S))[:, None, :]
    slopes = alibi_slopes().reshape(KV_HEADS, Q_PER_KV)
    bias = -slopes[:, :, None, None] * jnp.abs(rel).astype(jnp.float32)[None, None]
    s = jnp.where(mask[None, :, None, None], s + bias[None, None], NEG_INF)
    sink_l = jnp.broadcast_to(sink.astype(jnp.float32).reshape(KV_HEADS, Q_PER_KV)[None, None, :, :, None, None],
                              s.shape[:-1] + (1,))
    p = jax.nn.softmax(jnp.concatenate([s, sink_l], axis=-1), axis=-1)[..., :-1]
    o = jnp.einsum('bnkgqs,bnskd->bnqkgd', p.astype(vw.dtype), vw)
    return o.reshape(B, S, ATTN_W)


def spatial_gating(u, vg, w_s, b_s, g_norm):
    B, S, _ = u.shape
    nc = S // CHUNK
    u = jax.nn.gelu(u).reshape(B, S, SGU_GROUPS, SGU_DIM)
    vg = rms_norm(jax.nn.gelu(vg).reshape(B, S, SGU_GROUPS, SGU_DIM), g_norm.reshape(SGU_GROUPS, SGU_DIM))
    vc = vg.reshape(B, nc, CHUNK, SGU_GROUPS, SGU_DIM)
    mixed = jnp.einsum('gts,bnsgd->bntgd', w_s, vc) + b_s.T[None, None, :, :, None]
    return (u * mixed.reshape(B, S, SGU_GROUPS, SGU_DIM)).reshape(B, S, SGU_W)


def conformer_conv(a, gate, conv_w, conv_b, norm_g):
    h = a * jax.nn.sigmoid(gate)
    pad = CONV_WIDTH // 2
    h = lax.conv_general_dilated(h, conv_w.astype(h.dtype)[:, None, :], window_strides=(1,),
                                 padding=[(pad, pad)], dimension_numbers=('NWC', 'WIO', 'NWC'),
                                 feature_group_count=CONV_CH)
    h = h + conv_b
    return jax.nn.silu(rms_norm(h, norm_g))


def multiscale_pool(z, pool_w, pool_scale):
    B, S, _ = z.shape
    zf = z.reshape(B, S, POOL_GROUPS, POOL_DIM).astype(jnp.float32)
    cs = jnp.pad(jnp.cumsum(zf, axis=1), ((0, 0), (1, 0), (0, 0), (0, 0)))
    t = jnp.arange(S)
    means = []
    for g, w in enumerate(POOL_WINDOWS):
        lo = jnp.clip(t - w // 2, 0, S)
        hi = jnp.clip(t + w // 2, 0, S)
        csg = cs[:, :, g]
        sums = csg[:, hi] - csg[:, lo]
        means.append(sums / (hi - lo).astype(jnp.float32)[None, :, None])
    pooled = (jnp.stack(means, axis=2) - zf).astype(z.dtype)
    out = jnp.einsum('bsgc,gcd->bsgd', pooled, pool_w).reshape(B, S, POOL_W)
    return out * pool_scale


def expert_choice_moe(x, w_router, b_router, w_gate, w_up, w_down):
    B, S, D = x.shape
    n_tok = B * S
    cap = max(1, CAPACITY_FACTOR * n_tok // N_EXPERTS)
    xf = x.reshape(n_tok, D)
    logits = (xf @ w_router + b_router).astype(jnp.float32)
    aff = jax.nn.softmax(logits, axis=-1)
    gates, idx = lax.top_k(aff.T, cap)
    xe = xf[idx]
    h = jax.nn.silu(jnp.einsum('ecd,edf->ecf', xe, w_gate)) * jnp.einsum('ecd,edf->ecf', xe, w_up)
    ye = jnp.einsum('ecf,efd->ecd', h, w_down) * gates[..., None].astype(x.dtype)
    out = jnp.zeros((n_tok, D), x.dtype).at[idx.reshape(-1)].add(ye.reshape(-1, D))
    return out.reshape(B, S, D)


def encoder_trunk(x, norm_mix, norm_ffn, w_in_ab, w_out_ab, q_norm, k_norm, attn_sink, w_spatial,
                  b_spatial, sgu_norm, w_in_cd, w_out_cd, conv_w, conv_b, conv_norm, pool_w, pool_scale,
                  w_router, b_router, w_gate, w_up, w_down):
    for layer in range(DEPTH):
        h = rms_norm(x, norm_mix[layer])
        if layer % 2 == 0:
            i = layer // 2
            p = h @ w_in_ab[i]
            q = p[..., :ATTN_W]
            k = p[..., ATTN_W:ATTN_W + KV_W]
            v = p[..., ATTN_W + KV_W:ATTN_W + 2 * KV_W]
            u = p[..., ATTN_W + 2 * KV_W:ATTN_W + 2 * KV_W + SGU_W]
            vg = p[..., ATTN_W + 2 * KV_W + SGU_W:]
            a_out = windowed_gqa(q, k, v, q_norm[i], k_norm[i], attn_sink[i])
            b_out = spatial_gating(u, vg, w_spatial[i], b_spatial[i], sgu_norm[i])
            x = x + jnp.concatenate([a_out, b_out], axis=-1) @ w_out_ab[i]
        else:
            i = layer // 2
            p = h @ w_in_cd[i]
            ga = p[..., :CONV_CH]
            gg = p[..., CONV_CH:2 * CONV_CH]
            z = p[..., 2 * CONV_CH:]
            c_out = conformer_conv(ga, gg, conv_w[i], conv_b[i], conv_norm[i])
            d_out = multiscale_pool(z, pool_w[i], pool_scale[i])
            x = x + jnp.concatenate([c_out, d_out], axis=-1) @ w_out_cd[i]
        h = rms_norm(x, norm_ffn[layer])
        x = x + expert_choice_moe(h, w_router[layer], b_router[layer], w_gate[layer], w_up[layer], w_down[layer])
    return x


def setup_inputs(seed: int = 0) -> dict:
    key = jax.random.key(seed)
    ks = jax.random.split(key, 24)
    f32 = jnp.float32
    nrm = lambda k, shape, scale: jax.random.normal(k, shape, f32) * scale
    gain = lambda k, shape: 1.0 + 0.1 * jax.random.normal(k, shape, f32)
    return {
        "x_prompt": jax.random.normal(ks[0], (BATCH, SEQ, D_MODEL), f32),
        "x_sample": jax.random.normal(ks[1], (DEC_BATCH, DEC_SEQ, D_MODEL), f32),
        "norm_mix": gain(ks[2], (DEPTH, D_MODEL)),
        "norm_ffn": gain(ks[3], (DEPTH, D_MODEL)),
        "w_in_ab": nrm(ks[4], (N_EVEN, D_MODEL, IN_AB), D_MODEL ** -0.5),
        "w_out_ab": nrm(ks[5], (N_EVEN, MIX_AB, D_MODEL), MIX_AB ** -0.5),
        "q_norm": gain(ks[6], (N_EVEN, HEAD_DIM)),
        "k_norm": gain(ks[7], (N_EVEN, HEAD_DIM)),
        "attn_sink": nrm(ks[8], (N_EVEN, ATTN_HEADS), 1.0),
        "w_spatial": nrm(ks[9], (N_EVEN, SGU_GROUPS, CHUNK, CHUNK), CHUNK ** -0.5),
        "b_spatial": gain(ks[10], (N_EVEN, SGU_GROUPS, CHUNK)),
        "sgu_norm": gain(ks[11], (N_EVEN, SGU_W)),
        "w_in_cd": nrm(ks[12], (N_ODD, D_MODEL, IN_CD), D_MODEL ** -0.5),
        "w_out_cd": nrm(ks[13], (N_ODD, MIX_CD, D_MODEL), MIX_CD ** -0.5),
        "conv_w": nrm(ks[14], (N_ODD, CONV_WIDTH, CONV_CH), CONV_WIDTH ** -0.5),
        "conv_b": nrm(ks[15], (N_ODD, CONV_CH), 0.02),
        "conv_norm": gain(ks[16], (N_ODD, CONV_CH)),
        "pool_w": nrm(ks[17], (N_ODD, POOL_GROUPS, POOL_DIM, POOL_DIM), POOL_DIM ** -0.5),
        "pool_scale": gain(ks[18], (N_ODD, POOL_W)),
        "w_router": nrm(ks[19], (DEPTH, D_MODEL, N_EXPERTS), D_MODEL ** -0.5),
        "b_router": nrm(ks[20], (DEPTH, N_EXPERTS), 0.01),
        "w_gate": nrm(ks[21], (DEPTH, N_EXPERTS, D_MODEL, EXPERT_FF), D_MODEL ** -0.5),
        "w_up": nrm(ks[22], (DEPTH, N_EXPERTS, D_MODEL, EXPERT_FF), D_MODEL ** -0.5),
        "w_down": nrm(ks[23], (DEPTH, N_EXPERTS, EXPERT_FF, D_MODEL), EXPERT_FF ** -0.5),
    }


def reference(x_prompt, x_sample, norm_mix, norm_ffn, w_in_ab, w_out_ab, q_norm, k_norm, attn_sink,
              w_spatial, b_spatial, sgu_norm, w_in_cd, w_out_cd, conv_w, conv_b, conv_norm, pool_w,
              pool_scale, w_router, b_router, w_gate, w_up, w_down):
    y_prompt = encoder_trunk(x_prompt, norm_mix, norm_ffn, w_in_ab, w_out_ab, q_norm, k_norm, attn_sink,
                             w_spatial, b_spatial, sgu_norm, w_in_cd, w_out_cd, conv_w, conv_b, conv_norm,
                             pool_w, pool_scale, w_router, b_router, w_gate, w_up, w_down)
    y_sample = encoder_trunk(x_sample, norm_mix, norm_ffn, w_in_ab, w_out_ab, q_norm, k_norm, attn_sink,
                             w_spatial, b_spatial, sgu_norm, w_in_cd, w_out_cd, conv_w, conv_b, conv_norm,
                             pool_w, pool_scale, w_router, b_router, w_gate, w_up, w_down)
    return (y_prompt, y_sample)
```

```python
import functools

import jax
import jax.numpy as jnp
import numpy as np
from jax import lax
from jax.experimental import pallas as pl
from jax.experimental.pallas import tpu as pltpu

F32 = jnp.float32
BF16 = jnp.bfloat16

D_MODEL = 1024
HEAD_DIM = 64
ATTN_HEADS = 8
KV_HEADS = 2
WINDOW = 128
BLOCK = 128
ATTN_W = ATTN_HEADS * HEAD_DIM
KV_W = KV_HEADS * HEAD_DIM
SGU_GROUPS = 8
SGU_DIM = 64
SGU_W = SGU_GROUPS * SGU_DIM
CONV_CH = 512
CONV_WIDTH = 31
CONV_PAD = CONV_WIDTH // 2
POOL_GROUPS = 4
POOL_DIM = 128
POOL_W = POOL_GROUPS * POOL_DIM
POOL_WINDOWS = (2, 4, 8, 16)
N_EXPERTS = 16
EXPERT_FF = 1536
CAPACITY_FACTOR = 2
EPS = 1e-6
NEG_INF = -1e30
LANES = 128
HALO = 16

ROW_TILE = 512
MIX_TILE = 512
FFN_TILE = 512


def _dot(a, b):
    return jnp.dot(a, b, preferred_element_type=F32)


def _dot_nt(a, b):
    return lax.dot_general(a, b, (((1,), (1,)), ((), ())), preferred_element_type=F32)


def _norm_proj_kernel(x_ref, g_ref, w_ref, *out_refs, splits):
    x = x_ref[...]
    ms = jnp.mean(x * x, axis=-1, keepdims=True)
    h = (x * lax.rsqrt(ms + EPS) * g_ref[...]).astype(BF16)
    off = 0
    for o_ref, width in zip(out_refs, splits):
        o_ref[...] = _dot(h, w_ref[:, off:off + width]).astype(o_ref.dtype)
        off += width


def norm_proj(x2, gain, w_bf16, splits):
    n, d = x2.shape
    width = w_bf16.shape[1]
    tm = min(ROW_TILE, n)
    return pl.pallas_call(
        functools.partial(_norm_proj_kernel, splits=splits),
        grid=(n // tm,),
        in_specs=[
            pl.BlockSpec((tm, d), lambda i: (i, 0)),
            pl.BlockSpec((1, d), lambda i: (0, 0)),
            pl.BlockSpec((d, width), lambda i: (0, 0)),
        ],
        out_specs=[pl.BlockSpec((tm, s), lambda i: (i, 0)) for s in splits],
        out_shape=[jax.ShapeDtypeStruct((n, s), BF16) for s in splits],
        compiler_params=pltpu.CompilerParams(dimension_semantics=("parallel",)),
        name="norm_proj",
    )(x2, gain.reshape(1, d), w_bf16)


def _even_mixer_kernel(sink_ref, q_ref, kp_ref, k_ref, kn_ref, vp_ref, v_ref, vn_ref, u_ref, vg_ref,
                       bias_ref, qgain_ref, kgain_ref, ones_q_ref, ones_k_ref, w2_ref, bsp_ref,
                       sgn_ref, o_ref, *, n_tiles):
    i = pl.program_id(1)
    tq = q_ref.shape[0]
    nblk = tq // BLOCK
    lane = lax.broadcasted_iota(jnp.int32, (1, LANES), 1)
    lo_mask = (lane < HEAD_DIM).astype(F32)
    hi_mask = 1.0 - lo_mask

    qf = q_ref[...].astype(F32)
    q_ms = _dot((qf * qf).astype(BF16), ones_q_ref[...]) * (1.0 / HEAD_DIM)
    qn = (qf * lax.rsqrt(q_ms + EPS) * qgain_ref[...] * (HEAD_DIM ** -0.5)).astype(BF16)

    kf = jnp.concatenate([kp_ref[...], k_ref[...], kn_ref[...]], axis=0).astype(F32)
    k_ms = _dot((kf * kf).astype(BF16), ones_k_ref[...]) * (1.0 / HEAD_DIM)
    kn = kf * lax.rsqrt(k_ms + EPS) * kgain_ref[...]
    vf = jnp.concatenate([vp_ref[...], v_ref[...], vn_ref[...]], axis=0).astype(F32)
    kr = pltpu.roll(kn, HEAD_DIM, 1)
    vr = pltpu.roll(vf, HEAD_DIM, 1)
    k_lo = [(kn * lo_mask).astype(BF16), (kr * lo_mask).astype(BF16)]
    k_hi = [(kr * hi_mask).astype(BF16), (kn * hi_mask).astype(BF16)]
    v_lo = [(vf * lo_mask).astype(BF16), (vr * lo_mask).astype(BF16)]
    v_hi = [(vr * hi_mask).astype(BF16), (vf * hi_mask).astype(BF16)]

    col = lax.broadcasted_iota(jnp.int32, (1, 6 * BLOCK), 1)
    col_blk = (col // BLOCK) % 3
    row = lax.broadcasted_iota(jnp.int32, (2 * BLOCK, 1), 0)
    first_pair = row < BLOCK
    lane_lo = lane < HEAD_DIM

    uf = jax.nn.gelu(u_ref[...].astype(F32))
    gf = jax.nn.gelu(vg_ref[...].astype(F32))
    g_ms = _dot((gf * gf).astype(BF16), ones_q_ref[...]) * (1.0 / SGU_DIM)
    gn = gf * lax.rsqrt(g_ms + EPS) * sgn_ref[...]

    for j in range(nblk):
        r0 = j * BLOCK
        prev_ok = jnp.logical_or(i > 0, j > 0)
        next_ok = jnp.logical_or(i < n_tiles - 1, j < nblk - 1)
        col_ok = jnp.logical_and(jnp.logical_or(col_blk != 0, prev_ok),
                                 jnp.logical_or(col_blk != 2, next_ok))
        a_parts = []
        for g in range(KV_HEADS):
            kblk = jnp.concatenate([k_lo[g][r0:r0 + 3 * BLOCK], k_hi[g][r0:r0 + 3 * BLOCK]], axis=0)
            vblk = jnp.concatenate([v_lo[g][r0:r0 + 3 * BLOCK], v_hi[g][r0:r0 + 3 * BLOCK]], axis=0)
            c0 = g * 2 * LANES
            q2 = jnp.concatenate([qn[r0:r0 + BLOCK, c0:c0 + LANES],
                                  qn[r0:r0 + BLOCK, c0 + LANES:c0 + 2 * LANES]], axis=0)
            tab = bias_ref[g]
            s = _dot_nt(q2, kblk)
            ok = jnp.logical_and(tab > 0.5 * NEG_INF, col_ok)
            s = jnp.where(ok, s + tab, NEG_INF)
            sink_e = jnp.where(first_pair, sink_ref[4 * g], sink_ref[4 * g + 2])
            sink_o = jnp.where(first_pair, sink_ref[4 * g + 1], sink_ref[4 * g + 3])
            s_e = s[:, :3 * BLOCK]
            s_o = s[:, 3 * BLOCK:]
            m_e = jnp.maximum(jnp.max(s_e, axis=-1, keepdims=True), sink_e)
            m_o = jnp.maximum(jnp.max(s_o, axis=-1, keepdims=True), sink_o)
            p_e = jnp.exp(s_e - m_e)
            p_o = jnp.exp(s_o - m_o)
            d_e = jnp.sum(p_e, axis=-1, keepdims=True) + jnp.exp(sink_e - m_e)
            d_o = jnp.sum(p_o, axis=-1, keepdims=True) + jnp.exp(sink_o - m_o)
            p = jnp.concatenate([p_e, p_o], axis=1).astype(BF16)
            o = _dot(p, vblk)
            o = o * jnp.where(lane_lo, 1.0 / d_e, 1.0 / d_o)
            a_parts.append(o[:BLOCK])
            a_parts.append(o[BLOCK:])
        a_out = jnp.concatenate(a_parts, axis=1)

        gb = gn[r0:r0 + BLOCK]
        b_parts = []
        for pr in range(SGU_GROUPS // 2):
            gp = gb[:, pr * LANES:(pr + 1) * LANES]
            rhs = jnp.concatenate([(gp * lo_mask).astype(BF16), (gp * hi_mask).astype(BF16)], axis=0)
            b_parts.append(_dot(w2_ref[pr], rhs))
        mixed = jnp.concatenate(b_parts, axis=1) + bsp_ref[...]
        b_out = uf[r0:r0 + BLOCK] * mixed
        o_ref[r0:r0 + BLOCK, :] = jnp.concatenate([a_out, b_out], axis=1).astype(o_ref.dtype)


def _attn_bias_tables():
    slopes = np.exp2(-8.0 * np.arange(1, ATTN_HEADS + 1, dtype=np.float64) / ATTN_HEADS)
    t = np.arange(BLOCK)[:, None]
    c = np.arange(3 * BLOCK)[None, :]
    rel = np.abs(c - BLOCK - t)
    tabs = np.zeros((KV_HEADS, 2 * BLOCK, 6 * BLOCK), np.float32)
    for g in range(KV_HEADS):
        for pr in range(2):
            for half in range(2):
                h = 4 * g + 2 * pr + half
                tab = np.where(rel <= WINDOW, -slopes[h] * rel, NEG_INF)
                tabs[g, pr * BLOCK:(pr + 1) * BLOCK, half * 3 * BLOCK:(half + 1) * 3 * BLOCK] = tab
    return jnp.asarray(tabs)


def _block_ones(width, seg):
    idx = np.arange(width) // seg
    return jnp.asarray((idx[:, None] == idx[None, :]).astype(np.float32), dtype=BF16)


def even_mixer(q, k, v, u, vg, batch, seq, q_gain, k_gain, sink, w_spatial, b_spatial, sgu_norm):
    n = batch * seq
    tq = min(MIX_TILE, seq)
    n_tiles = seq // tq
    bpt = tq // BLOCK
    nb_total = n // BLOCK

    def main_map(b, i, *_):
        return (b * n_tiles + i, 0)

    def prev_map(b, i, *_):
        return (jnp.maximum((b * n_tiles + i) * bpt - 1, 0), 0)

    def next_map(b, i, *_):
        return (jnp.minimum((b * n_tiles + i + 1) * bpt, nb_total - 1), 0)

    def const2(b, i, *_):
        return (0, 0)

    def const3(b, i, *_):
        return (0, 0, 0)

    bias = _attn_bias_tables()
    ones_q = _block_ones(ATTN_W, HEAD_DIM)
    ones_k = _block_ones(KV_W, HEAD_DIM)
    qg = jnp.tile(q_gain, ATTN_HEADS).reshape(1, ATTN_W)
    kg = jnp.tile(k_gain, KV_HEADS).reshape(1, KV_W)
    w2 = w_spatial.reshape(SGU_GROUPS // 2, 2, BLOCK, BLOCK).transpose(0, 2, 1, 3)
    w2 = w2.reshape(SGU_GROUPS // 2, BLOCK, 2 * BLOCK).astype(BF16)
    bsp = jnp.repeat(b_spatial.T, SGU_DIM, axis=1)
    kernel = functools.partial(_even_mixer_kernel, n_tiles=n_tiles)
    grid_spec = pltpu.PrefetchScalarGridSpec(
        num_scalar_prefetch=1,
        grid=(batch, n_tiles),
        in_specs=[
            pl.BlockSpec((tq, ATTN_W), main_map),
            pl.BlockSpec((BLOCK, KV_W), prev_map),
            pl.BlockSpec((tq, KV_W), main_map),
            pl.BlockSpec((BLOCK, KV_W), next_map),
            pl.BlockSpec((BLOCK, KV_W), prev_map),
            pl.BlockSpec((tq, KV_W), main_map),
            pl.BlockSpec((BLOCK, KV_W), next_map),
            pl.BlockSpec((tq, SGU_W), main_map),
            pl.BlockSpec((tq, SGU_W), main_map),
            pl.BlockSpec((KV_HEADS, 2 * BLOCK, 6 * BLOCK), const3),
            pl.BlockSpec((1, ATTN_W), const2),
            pl.BlockSpec((1, KV_W), const2),
            pl.BlockSpec((ATTN_W, ATTN_W), const2),
            pl.BlockSpec((KV_W, KV_W), const2),
            pl.BlockSpec((SGU_GROUPS // 2, BLOCK, 2 * BLOCK), const3),
            pl.BlockSpec((BLOCK, SGU_W), const2),
            pl.BlockSpec((1, SGU_W), const2),
        ],
        out_specs=pl.BlockSpec((tq, D_MODEL), main_map),
    )
    return pl.pallas_call(
        kernel,
        grid_spec=grid_spec,
        out_shape=jax.ShapeDtypeStruct((n, D_MODEL), BF16),
        compiler_params=pltpu.CompilerParams(dimension_semantics=("parallel", "parallel")),
        name="even_mixer",
    )(sink.astype(F32), q, k, k, k, v, v, v, u, vg, bias, qg, kg, ones_q, ones_k, w2, bsp,
      sgu_norm.reshape(1, SGU_W))


def _odd_mixer_kernel(ap_ref, a_ref, an_ref, gp_ref, g_ref, gn_ref, zp_ref, z_ref, zn_ref,
                      cw_ref, cb_ref, cn_ref, pw_ref, ps_ref, o_ref, h_scr, z_scr, *, n_tiles, seq):
    i = pl.program_id(1)
    tq = a_ref.shape[0]
    prev_ok = (i > 0).astype(F32)
    next_ok = (i < n_tiles - 1).astype(F32)

    def glu(a, g):
        return a[...].astype(F32) * jax.nn.sigmoid(g[...].astype(F32))

    h_scr[0:HALO, :] = glu(ap_ref, gp_ref) * prev_ok
    h_scr[HALO:HALO + tq, :] = glu(a_ref, g_ref)
    h_scr[HALO + tq:, :] = glu(an_ref, gn_ref) * next_ok
    z_scr[0:HALO, :] = zp_ref[...].astype(F32) * prev_ok
    z_scr[HALO:HALO + tq, :] = z_ref[...].astype(F32)
    z_scr[HALO + tq:, :] = zn_ref[...].astype(F32) * next_ok

    rc = 64
    for c in range(tq // rc):
        r0 = c * rc
        acc = jnp.zeros((rc, CONV_CH), F32)
        for tap in range(CONV_WIDTH):
            start = r0 + HALO - CONV_PAD + tap
            acc = acc + cw_ref[tap:tap + 1, :] * h_scr[start:start + rc, :]
        acc = acc + cb_ref[...]
        ms = jnp.mean(acc * acc, axis=-1, keepdims=True)
        y = acc * lax.rsqrt(ms + EPS) * cn_ref[...]
        y = y * jax.nn.sigmoid(y)

        pos = (i * tq + r0 + lax.broadcasted_iota(jnp.int32, (rc, 1), 0))
        d_parts = []
        for g, w in enumerate(POOL_WINDOWS):
            l0 = g * POOL_DIM
            tot = jnp.zeros((rc, POOL_DIM), F32)
            for off in range(-(w // 2), w // 2):
                start = r0 + HALO + off
                tot = tot + z_scr[start:start + rc, l0:l0 + POOL_DIM]
            cnt = (jnp.minimum(pos + w // 2, seq) - jnp.maximum(pos - w // 2, 0)).astype(F32)
            zc = z_scr[r0 + HALO:r0 + HALO + rc, l0:l0 + POOL_DIM]
            pooled = (tot / cnt - zc).astype(BF16)
            d_parts.append(_dot(pooled, pw_ref[g]))
        d_out = jnp.concatenate(d_parts, axis=1) * ps_ref[...]
        o_ref[r0:r0 + rc, :] = jnp.concatenate([y, d_out], axis=1).astype(o_ref.dtype)


def odd_mixer(ga, gg, z, batch, seq, conv_w, conv_b, conv_norm, pool_w, pool_scale):
    n = batch * seq
    tq = min(MIX_TILE, seq)
    n_tiles = seq // tq
    hpt = tq // HALO
    nh_total = n // HALO

    def main_map(b, i):
        return (b * n_tiles + i, 0)

    def prev_map(b, i):
        return (jnp.maximum((b * n_tiles + i) * hpt - 1, 0), 0)

    def next_map(b, i):
        return (jnp.minimum((b * n_tiles + i + 1) * hpt, nh_total - 1), 0)

    def const2(b, i):
        return (0, 0)

    def const3(b, i):
        return (0, 0, 0)

    def halo_specs():
        return [pl.BlockSpec((HALO, CONV_CH), prev_map),
                pl.BlockSpec((tq, CONV_CH), main_map),
                pl.BlockSpec((HALO, CONV_CH), next_map)]

    kernel = functools.partial(_odd_mixer_kernel, n_tiles=n_tiles, seq=seq)
    return pl.pallas_call(
        kernel,
        grid=(batch, n_tiles),
        in_specs=halo_specs() + halo_specs() + halo_specs() + [
            pl.BlockSpec((CONV_WIDTH, CONV_CH), const2),
            pl.BlockSpec((1, CONV_CH), const2),
            pl.BlockSpec((1, CONV_CH), const2),
            pl.BlockSpec((POOL_GROUPS, POOL_DIM, POOL_DIM), const3),
            pl.BlockSpec((1, POOL_W), const2),
        ],
        out_specs=pl.BlockSpec((tq, D_MODEL), main_map),
        out_shape=jax.ShapeDtypeStruct((n, D_MODEL), BF16),
        scratch_shapes=[pltpu.VMEM((tq + 2 * HALO, CONV_CH), F32),
                        pltpu.VMEM((tq + 2 * HALO, POOL_W), F32)],
        compiler_params=pltpu.CompilerParams(dimension_semantics=("parallel", "parallel")),
        name="odd_mixer",
    )(ga, ga, ga, gg, gg, gg, z, z, z, conv_w, conv_b.reshape(1, CONV_CH),
      conv_norm.reshape(1, CONV_CH), pool_w.astype(BF16), pool_scale.reshape(1, POOL_W))


def _out_proj_kernel(x_ref, m_ref, w_ref, g_ref, wr_ref, br_ref, xo_ref, h_ref, aff_ref):
    xn = x_ref[...] + _dot(m_ref[...], w_ref[...])
    xo_ref[...] = xn
    ms = jnp.mean(xn * xn, axis=-1, keepdims=True)
    h = (xn * lax.rsqrt(ms + EPS) * g_ref[...]).astype(BF16)
    h_ref[...] = h
    logits = _dot(h, wr_ref[...]) + br_ref[...]
    m = jnp.max(logits, axis=-1, keepdims=True)
    e = jnp.exp(logits - m)
    aff_ref[...] = e / jnp.sum(e, axis=-1, keepdims=True)


def out_proj_router(x2, mix, w_out_bf16, norm_g, w_router, b_router):
    n, d = x2.shape
    tm = min(ROW_TILE, n)
    wr = jnp.zeros((d, LANES), F32).at[:, :N_EXPERTS].set(w_router).astype(BF16)
    br = jnp.full((1, LANES), NEG_INF, F32).at[0, :N_EXPERTS].set(b_router)
    return pl.pallas_call(
        _out_proj_kernel,
        grid=(n // tm,),
        in_specs=[
            pl.BlockSpec((tm, d), lambda i: (i, 0)),
            pl.BlockSpec((tm, d), lambda i: (i, 0)),
            pl.BlockSpec((d, d), lambda i: (0, 0)),
            pl.BlockSpec((1, d), lambda i: (0, 0)),
            pl.BlockSpec((d, LANES), lambda i: (0, 0)),
            pl.BlockSpec((1, LANES), lambda i: (0, 0)),
        ],
        out_specs=[
            pl.BlockSpec((tm, d), lambda i: (i, 0)),
            pl.BlockSpec((tm, d), lambda i: (i, 0)),
            pl.BlockSpec((tm, LANES), lambda i: (i, 0)),
        ],
        out_shape=[
            jax.ShapeDtypeStruct((n, d), F32),
            jax.ShapeDtypeStruct((n, d), BF16),
            jax.ShapeDtypeStruct((n, LANES), F32),
        ],
        compiler_params=pltpu.CompilerParams(dimension_semantics=("parallel",)),
        name="out_proj_router",
    )(x2, mix, w_out_bf16, norm_g.reshape(1, d), wr, br)


def _expert_ffn_kernel(x_ref, wg_ref, wu_ref, wd_ref, gate_ref, o_ref):
    x = x_ref[...]
    g = _dot(x, wg_ref[...])
    u = _dot(x, wu_ref[...])
    hmid = (g * jax.nn.sigmoid(g) * u).astype(BF16)
    y = _dot(hmid, wd_ref[...])
    gate = gate_ref[...]
    o_ref[...] = y * jnp.concatenate([gate] * (D_MODEL // LANES), axis=1)


def expert_ffn(xe, w_gate, w_up, w_down, gates_b):
    e, cap, d = xe.shape
    ff = w_gate.shape[-1]
    tm = min(FFN_TILE, cap)
    return pl.pallas_call(
        _expert_ffn_kernel,
        grid=(e, cap // tm),
        in_specs=[
            pl.BlockSpec((None, tm, d), lambda ei, r: (ei, r, 0)),
            pl.BlockSpec((None, d, ff), lambda ei, r: (ei, 0, 0)),
            pl.BlockSpec((None, d, ff), lambda ei, r: (ei, 0, 0)),
            pl.BlockSpec((None, ff, d), lambda ei, r: (ei, 0, 0)),
            pl.BlockSpec((None, tm, LANES), lambda ei, r: (ei, r, 0)),
        ],
        out_specs=pl.BlockSpec((None, tm, d), lambda ei, r: (ei, r, 0)),
        out_shape=jax.ShapeDtypeStruct((e, cap, d), F32),
        compiler_params=pltpu.CompilerParams(
            dimension_semantics=("parallel", "arbitrary"),
            vmem_limit_bytes=56 * 1024 * 1024),
        name="expert_ffn",
    )(xe, w_gate, w_up, w_down, gates_b)


def moe_layer(x_new, h, aff, w_gate, w_up, w_down):
    n, d = x_new.shape
    cap = max(1, CAPACITY_FACTOR * n // N_EXPERTS)
    gates, idx = lax.top_k(aff[:, :N_EXPERTS].T, cap)
    xe = h[idx]
    gates_b = jnp.broadcast_to(gates[..., None], (N_EXPERTS, cap, LANES))
    ye = expert_ffn(xe, w_gate, w_up, w_down, gates_b)
    return x_new.at[idx.reshape(-1)].add(ye.reshape(-1, d))


def _trunk(x, p, layer_weights):
    batch, seq, d = x.shape
    x2 = x.reshape(batch * seq, d)
    for layer, lw in enumerate(layer_weights):
        i = layer // 2
        if layer % 2 == 0:
            q, k, v, u, vg = norm_proj(x2, p["norm_mix"][layer], lw["w_in"],
                                       (ATTN_W, KV_W, KV_W, SGU_W, SGU_W))
            mix = even_mixer(q, k, v, u, vg, batch, seq, p["q_norm"][i], p["k_norm"][i],
                             p["attn_sink"][i], p["w_spatial"][i], p["b_spatial"][i], p["sgu_norm"][i])
        else:
            ga, gg, z = norm_proj(x2, p["norm_mix"][layer], lw["w_in"], (CONV_CH, CONV_CH, POOL_W))
            mix = odd_mixer(ga, gg, z, batch, seq, p["conv_w"][i], p["conv_b"][i], p["conv_norm"][i],
                            p["pool_w"][i], p["pool_scale"][i])
        x_new, h, aff = out_proj_router(x2, mix, lw["w_out"], p["norm_ffn"][layer],
                                        p["w_router"][layer], p["b_router"][layer])
        x2 = moe_layer(x_new, h, aff, lw["w_gate"], lw["w_up"], lw["w_down"])
    return x2.reshape(batch, seq, d)


def kernel(x_prompt, x_sample, norm_mix, norm_ffn, w_in_ab, w_out_ab, q_norm, k_norm, attn_sink, w_spatial, b_spatial, sgu_norm, w_in_cd, w_out_cd, conv_w, conv_b, conv_norm, pool_w, pool_scale, w_router, b_router, w_gate, w_up, w_down):
    p = dict(norm_mix=norm_mix, norm_ffn=norm_ffn, q_norm=q_norm, k_norm=k_norm, attn_sink=attn_sink,
             w_spatial=w_spatial, b_spatial=b_spatial, sgu_norm=sgu_norm, conv_w=conv_w, conv_b=conv_b,
             conv_norm=conv_norm, pool_w=pool_w, pool_scale=pool_scale, w_router=w_router,
             b_router=b_router)
    depth = norm_mix.shape[0]
    layer_weights = []
    for layer in range(depth):
        i = layer // 2
        w_in, w_out = (w_in_ab, w_out_ab) if layer % 2 == 0 else (w_in_cd, w_out_cd)
        layer_weights.append(dict(
            w_in=w_in[i].astype(BF16), w_out=w_out[i].astype(BF16),
            w_gate=w_gate[layer].astype(BF16), w_up=w_up[layer].astype(BF16),
            w_down=w_down[layer].astype(BF16)))
    y_prompt = _trunk(x_prompt, p, layer_weights)
    y_sample = _trunk(x_sample, p, layer_weights)
    return (y_prompt, y_sample)
```

```python
import functools

import jax
import jax.numpy as jnp
import numpy as np
from jax import lax
from jax.experimental import pallas as pl
from jax.experimental.pallas import tpu as pltpu

F32 = jnp.float32
BF16 = jnp.bfloat16

D_MODEL = 1024
HEAD_DIM = 64
ATTN_HEADS = 8
KV_HEADS = 2
WINDOW = 128
BLOCK = 128
ATTN_W = ATTN_HEADS * HEAD_DIM
KV_W = KV_HEADS * HEAD_DIM
SGU_GROUPS = 8
SGU_DIM = 64
SGU_W = SGU_GROUPS * SGU_DIM
CONV_CH = 512
CONV_WIDTH = 31
CONV_PAD = CONV_WIDTH // 2
POOL_GROUPS = 4
POOL_DIM = 128
POOL_W = POOL_GROUPS * POOL_DIM
POOL_WINDOWS = (2, 4, 8, 16)
N_EXPERTS = 16
EXPERT_FF = 1536
CAPACITY_FACTOR = 2
EPS = 1e-6
NEG_INF = -1e30
LANES = 128
HALO = 16

ROW_TILE = 512
MIX_TILE = 512
FFN_TILE = 512


def _dot(a, b):
    return jnp.dot(a, b, preferred_element_type=F32)


def _dot_nt(a, b):
    return lax.dot_general(a, b, (((1,), (1,)), ((), ())), preferred_element_type=F32)


def _norm_proj_kernel(x_ref, g_ref, w_ref, *out_refs, splits):
    x = x_ref[...]
    ms = jnp.mean(x * x, axis=-1, keepdims=True)
    h = (x * lax.rsqrt(ms + EPS) * g_ref[...]).astype(BF16)
    off = 0
    for o_ref, width in zip(out_refs, splits):
        o_ref[...] = _dot(h, w_ref[:, off:off + width]).astype(o_ref.dtype)
        off += width


def norm_proj(x2, gain, w_bf16, splits):
    n, d = x2.shape
    width = w_bf16.shape[1]
    tm = min(ROW_TILE, n)
    return pl.pallas_call(
        functools.partial(_norm_proj_kernel, splits=splits),
        grid=(n // tm,),
        in_specs=[
            pl.BlockSpec((tm, d), lambda i: (i, 0)),
            pl.BlockSpec((1, d), lambda i: (0, 0)),
            pl.BlockSpec((d, width), lambda i: (0, 0)),
        ],
        out_specs=[pl.BlockSpec((tm, s), lambda i: (i, 0)) for s in splits],
        out_shape=[jax.ShapeDtypeStruct((n, s), BF16) for s in splits],
        compiler_params=pltpu.CompilerParams(dimension_semantics=("parallel",)),
        name="norm_proj",
    )(x2, gain.reshape(1, d), w_bf16)


def _even_mixer_kernel(sink_ref, q_ref, kp_ref, k_ref, kn_ref, vp_ref, v_ref, vn_ref, u_ref, vg_ref,
                       bias_ref, qgain_ref, kgain_ref, ones_q_ref, ones_k_ref, w2_ref, bsp_ref,
                       sgn_ref, o_ref, *, n_tiles):
    i = pl.program_id(1)
    tq = q_ref.shape[0]
    nblk = tq // BLOCK
    lane = lax.broadcasted_iota(jnp.int32, (1, LANES), 1)
    lo_mask = (lane < HEAD_DIM).astype(F32)
    hi_mask = 1.0 - lo_mask

    qf = q_ref[...].astype(F32)
    q_ms = _dot((qf * qf).astype(BF16), ones_q_ref[...]) * (1.0 / HEAD_DIM)
    qn = (qf * lax.rsqrt(q_ms + EPS) * qgain_ref[...] * (HEAD_DIM ** -0.5)).astype(BF16)

    kf = jnp.concatenate([kp_ref[...], k_ref[...], kn_ref[...]], axis=0).astype(F32)
    k_ms = _dot((kf * kf).astype(BF16), ones_k_ref[...]) * (1.0 / HEAD_DIM)
    kn = kf * lax.rsqrt(k_ms + EPS) * kgain_ref[...]
    vf = jnp.concatenate([vp_ref[...], v_ref[...], vn_ref[...]], axis=0).astype(F32)
    kr = pltpu.roll(kn, HEAD_DIM, 1)
    vr = pltpu.roll(vf, HEAD_DIM, 1)
    k_lo = [(kn * lo_mask).astype(BF16), (kr * lo_mask).astype(BF16)]
    k_hi = [(kr * hi_mask).astype(BF16), (kn * hi_mask).astype(BF16)]
    v_lo = [(vf * lo_mask).astype(BF16), (vr * lo_mask).astype(BF16)]
    v_hi = [(vr * hi_mask).astype(BF16), (vf * hi_mask).astype(BF16)]

    col = lax.broadcasted_iota(jnp.int32, (1, 6 * BLOCK), 1)
    col_blk = (col // BLOCK) % 3
    row = lax.broadcasted_iota(jnp.int32, (2 * BLOCK, 1), 0)
    first_pair = row < BLOCK
    lane_lo = lane < HEAD_DIM

    uf = jax.nn.gelu(u_ref[...].astype(F32))
    gf = jax.nn.gelu(vg_ref[...].astype(F32))
    g_ms = _dot((gf * gf).astype(BF16), ones_q_ref[...]) * (1.0 / SGU_DIM)
    gn = gf * lax.rsqrt(g_ms + EPS) * sgn_ref[...]

    for j in range(nblk):
        r0 = j * BLOCK
        prev_ok = jnp.logical_or(i > 0, j > 0)
        next_ok = jnp.logical_or(i < n_tiles - 1, j < nblk - 1)
        col_ok = jnp.logical_and(jnp.logical_or(col_blk != 0, prev_ok),
                                 jnp.logical_or(col_blk != 2, next_ok))
        a_parts = []
        for g in range(KV_HEADS):
            kblk = jnp.concatenate([k_lo[g][r0:r0 + 3 * BLOCK], k_hi[g][r0:r0 + 3 * BLOCK]], axis=0)
            vblk = jnp.concatenate([v_lo[g][r0:r0 + 3 * BLOCK], v_hi[g][r0:r0 + 3 * BLOCK]], axis=0)
            c0 = g * 2 * LANES
            q2 = jnp.concatenate([qn[r0:r0 + BLOCK, c0:c0 + LANES],
                                  qn[r0:r0 + BLOCK, c0 + LANES:c0 + 2 * LANES]], axis=0)
            tab = bias_ref[g]
            s = _dot_nt(q2, kblk)
            ok = jnp.logical_and(tab > 0.5 * NEG_INF, col_ok)
            s = jnp.where(ok, s + tab, NEG_INF)
            sink_e = jnp.where(first_pair, sink_ref[4 * g], sink_ref[4 * g + 2])
            sink_o = jnp.where(first_pair, sink_ref[4 * g + 1], sink_ref[4 * g + 3])
            s_e = s[:, :3 * BLOCK]
            s_o = s[:, 3 * BLOCK:]
            m_e = jnp.maximum(jnp.max(s_e, axis=-1, keepdims=True), sink_e)
            m_o = jnp.maximum(jnp.max(s_o, axis=-1, keepdims=True), sink_o)
            p_e = jnp.exp(s_e - m_e)
            p_o = jnp.exp(s_o - m_o)
            d_e = jnp.sum(p_e, axis=-1, keepdims=True) + jnp.exp(sink_e - m_e)
            d_o = jnp.sum(p_o, axis=-1, keepdims=True) + jnp.exp(sink_o - m_o)
            p = jnp.concatenate([p_e, p_o], axis=1).astype(BF16)
            o = _dot(p, vblk)
            o = o * jnp.where(lane_lo, 1.0 / d_e, 1.0 / d_o)
            a_parts.append(o[:BLOCK])
            a_parts.append(o[BLOCK:])
        a_out = jnp.concatenate(a_parts, axis=1)

        gb = gn[r0:r0 + BLOCK]
        b_parts = []
        for pr in range(SGU_GROUPS // 2):
            gp = gb[:, pr * LANES:(pr + 1) * LANES]
            rhs = jnp.concatenate([(gp * lo_mask).astype(BF16), (gp * hi_mask).astype(BF16)], axis=0)
            b_parts.append(_dot(w2_ref[pr], rhs))
        mixed = jnp.concatenate(b_parts, axis=1) + bsp_ref[...]
        b_out = uf[r0:r0 + BLOCK] * mixed
        o_ref[r0:r0 + BLOCK, :] = jnp.concatenate([a_out, b_out], axis=1).astype(o_ref.dtype)


def _attn_bias_tables():
    slopes = np.exp2(-8.0 * np.arange(1, ATTN_HEADS + 1, dtype=np.float64) / ATTN_HEADS)
    t = np.arange(BLOCK)[:, None]
    c = np.arange(3 * BLOCK)[None, :]
    rel = np.abs(c - BLOCK - t)
    tabs = np.zeros((KV_HEADS, 2 * BLOCK, 6 * BLOCK), np.float32)
    for g in range(KV_HEADS):
        for pr in range(2):
            for half in range(2):
                h = 4 * g + 2 * pr + half
                tab = np.where(rel <= WINDOW, -slopes[h] * rel, NEG_INF)
                tabs[g, pr * BLOCK:(pr + 1) * BLOCK, half * 3 * BLOCK:(half + 1) * 3 * BLOCK] = tab
    return jnp.asarray(tabs)


def _block_ones(width, seg):
    idx = np.arange(width) // seg
    return jnp.asarray((idx[:, None] == idx[None, :]).astype(np.float32), dtype=BF16)


def even_mixer(q, k, v, u, vg, batch, seq, q_gain, k_gain, sink, w_spatial, b_spatial, sgu_norm):
    n = batch * seq
    tq = min(MIX_TILE, seq)
    n_tiles = seq // tq
    bpt = tq // BLOCK
    nb_total = n // BLOCK

    def main_map(b, i, *_):
        return (b * n_tiles + i, 0)

    def prev_map(b, i, *_):
        return (jnp.maximum((b * n_tiles + i) * bpt - 1, 0), 0)

    def next_map(b, i, *_):
        return (jnp.minimum((b * n_tiles + i + 1) * bpt, nb_total - 1), 0)

    def const2(b, i, *_):
        return (0, 0)

    def const3(b, i, *_):
        return (0, 0, 0)

    bias = _attn_bias_tables()
    ones_q = _block_ones(ATTN_W, HEAD_DIM)
    ones_k = _block_ones(KV_W, HEAD_DIM)
    qg = jnp.tile(q_gain, ATTN_HEADS).reshape(1, ATTN_W)
    kg = jnp.tile(k_gain, KV_HEADS).reshape(1, KV_W)
    w2 = w_spatial.reshape(SGU_GROUPS // 2, 2, BLOCK, BLOCK).transpose(0, 2, 1, 3)
    w2 = w2.reshape(SGU_GROUPS // 2, BLOCK, 2 * BLOCK).astype(BF16)
    bsp = jnp.repeat(b_spatial.T, SGU_DIM, axis=1)
    kernel = functools.partial(_even_mixer_kernel, n_tiles=n_tiles)
    grid_spec = pltpu.PrefetchScalarGridSpec(
        num_scalar_prefetch=1,
        grid=(batch, n_tiles),
        in_specs=[
            pl.BlockSpec((tq, ATTN_W), main_map),
            pl.BlockSpec((BLOCK, KV_W), prev_map),
            pl.BlockSpec((tq, KV_W), main_map),
            pl.BlockSpec((BLOCK, KV_W), next_map),
            pl.BlockSpec((BLOCK, KV_W), prev_map),
            pl.BlockSpec((tq, KV_W), main_map),
            pl.BlockSpec((BLOCK, KV_W), next_map),
            pl.BlockSpec((tq, SGU_W), main_map),
            pl.BlockSpec((tq, SGU_W), main_map),
            pl.BlockSpec((KV_HEADS, 2 * BLOCK, 6 * BLOCK), const3),
            pl.BlockSpec((1, ATTN_W), const2),
            pl.BlockSpec((1, KV_W), const2),
            pl.BlockSpec((ATTN_W, ATTN_W), const2),
            pl.BlockSpec((KV_W, KV_W), const2),
            pl.BlockSpec((SGU_GROUPS // 2, BLOCK, 2 * BLOCK), const3),
            pl.BlockSpec((BLOCK, SGU_W), const2),
            pl.BlockSpec((1, SGU_W), const2),
        ],
        out_specs=pl.BlockSpec((tq, D_MODEL), main_map),
    )
    return pl.pallas_call(
        kernel,
        grid_spec=grid_spec,
        out_shape=jax.ShapeDtypeStruct((n, D_MODEL), BF16),
        compiler_params=pltpu.CompilerParams(dimension_semantics=("parallel", "parallel")),
        name="even_mixer",
    )(sink.astype(F32), q, k, k, k, v, v, v, u, vg, bias, qg, kg, ones_q, ones_k, w2, bsp,
      sgu_norm.reshape(1, SGU_W))


def _odd_mixer_kernel(ap_ref, a_ref, an_ref, gp_ref, g_ref, gn_ref, zp_ref, z_ref, zn_ref,
                      cw_ref, cb_ref, cn_ref, pw_ref, ps_ref, o_ref, h_scr, z_scr, *, n_tiles, seq):
    i = pl.program_id(1)
    tq = a_ref.shape[0]
    prev_ok = (i > 0).astype(F32)
    next_ok = (i < n_tiles - 1).astype(F32)

    def glu(a, g):
        return a[...].astype(F32) * jax.nn.sigmoid(g[...].astype(F32))

    h_scr[0:HALO, :] = glu(ap_ref, gp_ref) * prev_ok
    h_scr[HALO:HALO + tq, :] = glu(a_ref, g_ref)
    h_scr[HALO + tq:, :] = glu(an_ref, gn_ref) * next_ok
    z_scr[0:HALO, :] = zp_ref[...].astype(F32) * prev_ok
    z_scr[HALO:HALO + tq, :] = z_ref[...].astype(F32)
    z_scr[HALO + tq:, :] = zn_ref[...].astype(F32) * next_ok

    rc = 64
    for c in range(tq // rc):
        r0 = c * rc
        acc = jnp.zeros((rc, CONV_CH), F32)
        for tap in range(CONV_WIDTH):
            start = r0 + HALO - CONV_PAD + tap
            acc = acc + cw_ref[tap:tap + 1, :] * h_scr[start:start + rc, :]
        acc = acc + cb_ref[...]
        ms = jnp.mean(acc * acc, axis=-1, keepdims=True)
        y = acc * lax.rsqrt(ms + EPS) * cn_ref[...]
        y = y * jax.nn.sigmoid(y)

        pos = (i * tq + r0 + lax.broadcasted_iota(jnp.int32, (rc, 1), 0))
        d_parts = []
        for g, w in enumerate(POOL_WINDOWS):
            l0 = g * POOL_DIM
            tot = jnp.zeros((rc, POOL_DIM), F32)
            for off in range(-(w // 2), w // 2):
                start = r0 + HALO + off
                tot = tot + z_scr[start:start + rc, l0:l0 + POOL_DIM]
            cnt = (jnp.minimum(pos + w // 2, seq) - jnp.maximum(pos - w // 2, 0)).astype(F32)
            zc = z_scr[r0 + HALO:r0 + HALO + rc, l0:l0 + POOL_DIM]
            pooled = (tot / cnt - zc).astype(BF16)
            d_parts.append(_dot(pooled, pw_ref[g]))
        d_out = jnp.concatenate(d_parts, axis=1) * ps_ref[...]
        o_ref[r0:r0 + rc, :] = jnp.concatenate([y, d_out], axis=1).astype(o_ref.dtype)


def odd_mixer(ga, gg, z, batch, seq, conv_w, conv_b, conv_norm, pool_w, pool_scale):
    n = batch * seq
    tq = min(MIX_TILE, seq)
    n_tiles = seq // tq
    hpt = tq // HALO
    nh_total = n // HALO

    def main_map(b, i):
        return (b * n_tiles + i, 0)

    def prev_map(b, i):
        return (jnp.maximum((b * n_tiles + i) * hpt - 1, 0), 0)

    def next_map(b, i):
        return (jnp.minimum((b * n_tiles + i + 1) * hpt, nh_total - 1), 0)

    def const2(b, i):
        return (0, 0)

    def const3(b, i):
        return (0, 0, 0)

    def halo_specs():
        return [pl.BlockSpec((HALO, CONV_CH), prev_map),
                pl.BlockSpec((tq, CONV_CH), main_map),
                pl.BlockSpec((HALO, CONV_CH), next_map)]

    kernel = functools.partial(_odd_mixer_kernel, n_tiles=n_tiles, seq=seq)
    return pl.pallas_call(
        kernel,
        grid=(batch, n_tiles),
        in_specs=halo_specs() + halo_specs() + halo_specs() + [
            pl.BlockSpec((CONV_WIDTH, CONV_CH), const2),
            pl.BlockSpec((1, CONV_CH), const2),
            pl.BlockSpec((1, CONV_CH), const2),
            pl.BlockSpec((POOL_GROUPS, POOL_DIM, POOL_DIM), const3),
            pl.BlockSpec((1, POOL_W), const2),
        ],
        out_specs=pl.BlockSpec((tq, D_MODEL), main_map),
        out_shape=jax.ShapeDtypeStruct((n, D_MODEL), BF16),
        scratch_shapes=[pltpu.VMEM((tq + 2 * HALO, CONV_CH), F32),
                        pltpu.VMEM((tq + 2 * HALO, POOL_W), F32)],
        compiler_params=pltpu.CompilerParams(dimension_semantics=("parallel", "parallel")),
        name="odd_mixer",
    )(ga, ga, ga, gg, gg, gg, z, z, z, conv_w, conv_b.reshape(1, CONV_CH),
      conv_norm.reshape(1, CONV_CH), pool_w.astype(BF16), pool_scale.reshape(1, POOL_W))


def _out_proj_kernel(x_ref, m_ref, w_ref, g_ref, wrt_ref, br_ref, xo_ref, h_ref, afft_ref):
    xn = x_ref[...] + _dot(m_ref[...], w_ref[...])
    xo_ref[...] = xn
    ms = jnp.mean(xn * xn, axis=-1, keepdims=True)
    h = xn * lax.rsqrt(ms + EPS) * g_ref[...]
    h_ref[...] = h
    logits = _dot_nt(wrt_ref[...], h.astype(BF16)) + br_ref[...]
    m = jnp.max(logits, axis=0, keepdims=True)
    e = jnp.exp(logits - m)
    afft_ref[...] = e / jnp.sum(e, axis=0, keepdims=True)


def out_proj_router(x2, mix, w_out_bf16, norm_g, w_router, b_router):
    n, d = x2.shape
    tm = min(ROW_TILE, n)
    return pl.pallas_call(
        _out_proj_kernel,
        grid=(n // tm,),
        in_specs=[
            pl.BlockSpec((tm, d), lambda i: (i, 0)),
            pl.BlockSpec((tm, d), lambda i: (i, 0)),
            pl.BlockSpec((d, d), lambda i: (0, 0)),
            pl.BlockSpec((1, d), lambda i: (0, 0)),
            pl.BlockSpec((N_EXPERTS, d), lambda i: (0, 0)),
            pl.BlockSpec((N_EXPERTS, 1), lambda i: (0, 0)),
        ],
        out_specs=[
            pl.BlockSpec((tm, d), lambda i: (i, 0)),
            pl.BlockSpec((tm, d), lambda i: (i, 0)),
            pl.BlockSpec((N_EXPERTS, tm), lambda i: (0, i)),
        ],
        out_shape=[
            jax.ShapeDtypeStruct((n, d), F32),
            jax.ShapeDtypeStruct((n, d), F32),
            jax.ShapeDtypeStruct((N_EXPERTS, n), F32),
        ],
        compiler_params=pltpu.CompilerParams(dimension_semantics=("parallel",)),
        name="out_proj_router",
    )(x2, mix, w_out_bf16, norm_g.reshape(1, d), w_router.T.astype(BF16),
      b_router.reshape(N_EXPERTS, 1))


def _expert_ffn_kernel(x_ref, wg_ref, wu_ref, wd_ref, gate_ref, o_ref):
    x = x_ref[...].astype(BF16)
    g = _dot(x, wg_ref[...])
    u = _dot(x, wu_ref[...])
    hmid = (g * jax.nn.sigmoid(g) * u).astype(BF16)
    y = _dot(hmid, wd_ref[...])
    gate = gate_ref[...]
    o_ref[...] = y * jnp.concatenate([gate] * (D_MODEL // LANES), axis=1)


def expert_ffn(xe, w_gate, w_up, w_down, gates_b):
    e, cap, d = xe.shape
    ff = w_gate.shape[-1]
    tm = min(FFN_TILE, cap)
    return pl.pallas_call(
        _expert_ffn_kernel,
        grid=(e, cap // tm),
        in_specs=[
            pl.BlockSpec((None, tm, d), lambda ei, r: (ei, r, 0)),
            pl.BlockSpec((None, d, ff), lambda ei, r: (ei, 0, 0)),
            pl.BlockSpec((None, d, ff), lambda ei, r: (ei, 0, 0)),
            pl.BlockSpec((None, ff, d), lambda ei, r: (ei, 0, 0)),
            pl.BlockSpec((None, tm, LANES), lambda ei, r: (ei, r, 0)),
        ],
        out_specs=pl.BlockSpec((None, tm, d), lambda ei, r: (ei, r, 0)),
        out_shape=jax.ShapeDtypeStruct((e, cap, d), F32),
        compiler_params=pltpu.CompilerParams(
            dimension_semantics=("parallel", "arbitrary"),
            vmem_limit_bytes=56 * 1024 * 1024),
        name="expert_ffn",
    )(xe, w_gate, w_up, w_down, gates_b)


def _tri(shape, fn):
    r = lax.broadcasted_iota(jnp.int32, shape, 0)
    c = lax.broadcasted_iota(jnp.int32, shape, 1)
    return fn(r, c).astype(BF16)


def _route_kernel(aff_ref, gm_ref, rr_ref, carry_ref, p_ref, pe_ref, pblk_ref, gt_scr, eq_scr, need_scr,
                  *, cap):
    ne, nb, _ = aff_ref.shape
    bits = pltpu.bitcast(aff_ref[...], jnp.int32)

    def count(flags):
        return jnp.sum(jnp.sum(flags.astype(F32), axis=1, keepdims=True), axis=2, keepdims=True)

    def bit_step(i, thr):
        cand = thr | (jnp.int32(1) << (30 - i))
        return jnp.where(count(bits >= cand) >= cap, cand, thr)

    thr = lax.fori_loop(0, 31, bit_step, jnp.zeros((ne, 1, 1), jnp.int32))
    gt = bits > thr
    gt_scr[...] = gt.astype(F32)
    eq_scr[...] = (bits == thr).astype(F32)
    need_scr[...] = jnp.broadcast_to(cap - count(gt), (ne, 1, LANES))

    u_incl = _tri((LANES, LANES), lambda r, c: r <= c)
    ones_l = jnp.ones((LANES, LANES), BF16)
    l_strict = _tri((nb, nb), lambda r, c: c < r)
    u_strict = _tri((nb, nb), lambda r, c: r < c)
    ones_8 = jnp.ones((8, LANES), BF16)

    def prefix(m):
        mb = m.astype(BF16)
        tot = _dot(mb, ones_l)
        carry = _dot(l_strict, tot.astype(BF16))
        return _dot(mb, u_incl) + carry - m, mb

    def per_expert(e, state):
        run, psum, pblk = state
        eq = eq_scr[e]
        eq_rank, _ = prefix(eq)
        mask = jnp.maximum(gt_scr[e], eq * (eq_rank < need_scr[e]).astype(F32))
        g_excl, mb = prefix(mask)
        gm_ref[e] = jnp.where(mask > 0.0, g_excl + 1.0, 0.5)
        rr_ref[e] = run
        tot_row = _dot_nt(ones_8, mb)
        carry_row = _dot(tot_row.astype(BF16), u_strict)[0:1]
        carry_ref[pl.ds(e, 1), :] = carry_row.astype(jnp.int32)
        return run + mask, psum + g_excl, pblk + carry_row

    zeros = jnp.zeros((nb, LANES), F32)
    cnt, p_tok, pblk = lax.fori_loop(0, ne, per_expert, (zeros, zeros, jnp.zeros((1, nb), F32)))
    p_ref[...] = p_tok
    pe_ref[...] = p_tok + cnt
    pblk_ref[...] = pblk.astype(jnp.int32)

    def add_base(e, _):
        rr_ref[e] = rr_ref[e] + p_tok
        return 0

    lax.fori_loop(0, ne, add_base, 0)


def route(aff3, cap):
    ne, nb, _ = aff3.shape
    full3 = jax.ShapeDtypeStruct((ne, nb, LANES), F32)
    tok = jax.ShapeDtypeStruct((nb, LANES), F32)
    return pl.pallas_call(
        functools.partial(_route_kernel, cap=cap),
        out_shape=[full3, full3, jax.ShapeDtypeStruct((ne, nb), jnp.int32), tok, tok,
                   jax.ShapeDtypeStruct((1, nb), jnp.int32)],
        scratch_shapes=[pltpu.VMEM((ne, nb, LANES), F32), pltpu.VMEM((ne, nb, LANES), F32),
                        pltpu.VMEM((ne, 1, LANES), F32)],
        compiler_params=pltpu.CompilerParams(vmem_limit_bytes=48 * 1024 * 1024),
        name="route",
    )(aff3)


HALF = 64


def _invert_kernel(carry_ref, gm_ref, rr_ref, aff_ref, idx_ref, gate_ref, rd_ref, *, nb, ct):
    e = pl.program_id(0)
    lane_f = lax.broadcasted_iota(jnp.int32, (1, LANES), 1).astype(F32)
    lane_i = lax.broadcasted_iota(jnp.int32, (HALF, LANES), 1)
    sub_i = lax.broadcasted_iota(jnp.int32, (HALF, LANES), 0)
    sub_f = lax.broadcasted_iota(jnp.int32, (HALF, 1), 0).astype(F32)

    def block_start(b):
        return carry_ref[e, jnp.minimum(b, nb - 1)]

    def half_tile(hh, state):
        ptr, row_i, row_g, row_r = state
        s0 = hh * HALF
        ptr = lax.while_loop(lambda q: jnp.logical_and(q + 1 < nb, block_start(q + 1) <= s0),
                             lambda q: q + 1, ptr)
        target = sub_f + jnp.asarray(s0 + 1, F32)

        def cond(st):
            return jnp.logical_and(st[0] < nb, block_start(st[0]) < s0 + HALF)

        def body(st):
            b, a_i, a_g, a_r = st
            hit = gm_ref[pl.ds(b, 1), :] == target
            tok = lane_f + jnp.asarray(b * LANES, F32)
            a_i = a_i + jnp.where(hit, tok, 0.0)
            a_g = a_g + jnp.where(hit, aff_ref[pl.ds(b, 1), :], 0.0)
            a_r = a_r + jnp.where(hit, rr_ref[pl.ds(b, 1), :], 0.0)
            return b + 1, a_i, a_g, a_r

        z = jnp.zeros((HALF, LANES), F32)
        _, a_i, a_g, a_r = lax.while_loop(cond, body, (ptr, z, z, z))
        odd = hh % 2
        diag = sub_i + odd * HALF == lane_i

        def to_row(a):
            col = jnp.sum(a, axis=1, keepdims=True)
            return jnp.sum(jnp.where(diag, col, 0.0), axis=0, keepdims=True)

        row_i = row_i + to_row(a_i)
        row_g = row_g + to_row(a_g)
        row_r = row_r + to_row(a_r)

        @pl.when(odd == 1)
        def _():
            j = hh // 2
            idx_ref[pl.ds(j, 1), :] = row_i.astype(jnp.int32)
            gate_ref[pl.ds(j, 1), :] = row_g
            rd_ref[pl.ds(j, 1), :] = row_r.astype(jnp.int32)

        keep = jnp.asarray(1 - odd, F32)
        return ptr, row_i * keep, row_g * keep, row_r * keep

    zr = jnp.zeros((1, LANES), F32)
    lax.fori_loop(0, 2 * ct, half_tile, (jnp.int32(0), zr, zr, zr))


def invert(carry, gm, rr, aff3, cap):
    ne, nb, _ = aff3.shape
    ct = cap // LANES
    kernel = functools.partial(_invert_kernel, nb=nb, ct=ct)
    tok_spec = pl.BlockSpec((None, nb, LANES), lambda e, *_: (e, 0, 0))
    slot_spec = pl.BlockSpec((None, ct, LANES), lambda e, *_: (e, 0, 0))
    grid_spec = pltpu.PrefetchScalarGridSpec(
        num_scalar_prefetch=1, grid=(ne,),
        in_specs=[tok_spec, tok_spec, tok_spec],
        out_specs=[slot_spec, slot_spec, slot_spec])
    return pl.pallas_call(
        kernel, grid_spec=grid_spec,
        out_shape=[jax.ShapeDtypeStruct((ne, ct, LANES), jnp.int32),
                   jax.ShapeDtypeStruct((ne, ct, LANES), F32),
                   jax.ShapeDtypeStruct((ne, ct, LANES), jnp.int32)],
        compiler_params=pltpu.CompilerParams(dimension_semantics=("arbitrary",)),
        name="invert",
    )(carry, gm, rr, aff3)


PERM_CHUNK = 2048


def _permute_kernel(idx_ref, src_hbm, dst_hbm, sems, *, gather, chunk):
    c = pl.program_id(0)
    n_chunks = pl.num_programs(0)
    base = c * chunk

    def row_copy(i, slot):
        r = idx_ref[base + i]
        s_row, d_row = (r, base + i) if gather else (base + i, r)
        return pltpu.make_async_copy(src_hbm.at[pl.ds(s_row, 1), :], dst_hbm.at[pl.ds(d_row, 1), :],
                                     sems.at[slot])

    def chunk_wait(cc):
        pltpu.make_async_copy(src_hbm.at[pl.ds(0, chunk), :], dst_hbm.at[pl.ds(0, chunk), :],
                              sems.at[cc % 2]).wait()

    def issue(i, _):
        row_copy(i, c % 2).start()
        return 0

    lax.fori_loop(0, chunk, issue, 0, unroll=8)

    @pl.when(c > 0)
    def _():
        chunk_wait(c - 1)

    @pl.when(c == n_chunks - 1)
    def _():
        chunk_wait(c)


def permute_rows(idx, src, *, gather):
    m = idx.shape[0]
    d = src.shape[1]
    chunk = min(PERM_CHUNK, m)
    grid_spec = pltpu.PrefetchScalarGridSpec(
        num_scalar_prefetch=1, grid=(m // chunk,),
        in_specs=[pl.BlockSpec(memory_space=pl.ANY)],
        out_specs=pl.BlockSpec(memory_space=pl.ANY),
        scratch_shapes=[pltpu.SemaphoreType.DMA((2,))])
    return pl.pallas_call(
        functools.partial(_permute_kernel, gather=gather, chunk=chunk),
        grid_spec=grid_spec,
        out_shape=jax.ShapeDtypeStruct((m, d), src.dtype),
        compiler_params=pltpu.CompilerParams(dimension_semantics=("arbitrary",), has_side_effects=True),
        name="gather_rows" if gather else "scatter_rows",
    )(idx, src)


PAIR_CHUNK = 256


def _combine_kernel(pblk_ref, x_ref, p_ref, pe_ref, z_hbm, o_ref, zbuf, acc, sems, par):
    b = pl.program_id(0)
    nbk = pl.num_programs(0)

    def chunks_of(t):
        lo = pblk_ref[t]
        hi = pblk_ref[t + 1]
        first = lo // PAIR_CHUNK
        n = jnp.where(hi > lo, (hi - 1) // PAIR_CHUNK - first + 1, 0)
        return first, n

    def chunk_copy(cidx, slot):
        return pltpu.make_async_copy(z_hbm.at[pl.ds(cidx * PAIR_CHUNK, PAIR_CHUNK), :], zbuf.at[slot],
                                     sems.at[slot])

    first, n = chunks_of(b)

    @pl.when(b == 0)
    def _():
        par[0] = 0

        @pl.when(n > 0)
        def _():
            chunk_copy(first, 0).start()

    slot0 = par[0]
    sub = lax.broadcasted_iota(jnp.int32, (LANES, LANES), 0)
    lan = lax.broadcasted_iota(jnp.int32, (LANES, LANES), 1)
    diag = sub == lan
    p_col = jnp.sum(jnp.where(diag, p_ref[...], 0.0), axis=1, keepdims=True)
    pe_col = jnp.sum(jnp.where(diag, pe_ref[...], 0.0), axis=1, keepdims=True)
    zlane = lax.broadcasted_iota(jnp.int32, (1, PAIR_CHUNK), 1)
    acc[...] = jnp.zeros_like(acc)

    def body(k, _):
        slot = (slot0 + k) % 2
        chunk_copy(first + k, slot).wait()

        @pl.when(k + 1 < n)
        def _():
            chunk_copy(first + k + 1, 1 - slot).start()

        zrow = (zlane + (first + k) * PAIR_CHUNK).astype(F32)
        sel = jnp.logical_and(p_col <= zrow, zrow < pe_col).astype(BF16)
        acc[...] += _dot(sel, zbuf[slot].astype(BF16))
        return 0

    lax.fori_loop(0, n, body, 0)
    o_ref[...] = x_ref[...] + acc[...]
    new_par = (slot0 + n) % 2
    par[0] = new_par

    @pl.when(b + 1 < nbk)
    def _():
        nfirst, nn = chunks_of(jnp.minimum(b + 1, nbk - 1))

        @pl.when(nn > 0)
        def _():
            chunk_copy(nfirst, new_par).start()


def combine(pblk_full, x_new, p3, pe3, z):
    n, d = x_new.shape
    nb = n // LANES
    tok_spec = pl.BlockSpec((None, 1, LANES), lambda b, *_: (b, 0, 0))
    grid_spec = pltpu.PrefetchScalarGridSpec(
        num_scalar_prefetch=1, grid=(nb,),
        in_specs=[pl.BlockSpec((LANES, d), lambda b, *_: (b, 0)), tok_spec, tok_spec,
                  pl.BlockSpec(memory_space=pl.ANY)],
        out_specs=pl.BlockSpec((LANES, d), lambda b, *_: (b, 0)),
        scratch_shapes=[pltpu.VMEM((2, PAIR_CHUNK, d), F32), pltpu.VMEM((LANES, d), F32),
                        pltpu.SemaphoreType.DMA((2,)), pltpu.SMEM((1,), jnp.int32)])
    return pl.pallas_call(
        _combine_kernel, grid_spec=grid_spec,
        out_shape=jax.ShapeDtypeStruct((n, d), F32),
        compiler_params=pltpu.CompilerParams(dimension_semantics=("arbitrary",)),
        name="combine",
    )(pblk_full, x_new, p3.reshape(nb, 1, LANES), pe3.reshape(nb, 1, LANES), z)


def moe_layer(x_new, h, afft, w_gate, w_up, w_down):
    n, d = x_new.shape
    nb = n // LANES
    cap = max(1, CAPACITY_FACTOR * n // N_EXPERTS)
    aff3 = afft.reshape(N_EXPERTS, nb, LANES)
    gm, rr, carry, p3, pe3, pblk = route(aff3, cap)
    idx, gates, rdst = invert(carry, gm, rr, aff3, cap)
    xe = permute_rows(idx.reshape(-1), h, gather=True)
    gates_b = jnp.broadcast_to(gates.reshape(N_EXPERTS, cap, 1), (N_EXPERTS, cap, LANES))
    ye = expert_ffn(xe.reshape(N_EXPERTS, cap, d), w_gate, w_up, w_down, gates_b)
    z = permute_rows(rdst.reshape(-1), ye.reshape(N_EXPERTS * cap, d), gather=False)
    pblk_full = jnp.concatenate([pblk.reshape(-1), jnp.full((1,), N_EXPERTS * cap, jnp.int32)])
    return combine(pblk_full, x_new, p3, pe3, z)


def _trunk(x, p, layer_weights):
    batch, seq, d = x.shape
    x2 = x.reshape(batch * seq, d)
    for layer, lw in enumerate(layer_weights):
        i = layer // 2
        if layer % 2 == 0:
            q, k, v, u, vg = norm_proj(x2, p["norm_mix"][layer], lw["w_in"],
                                       (ATTN_W, KV_W, KV_W, SGU_W, SGU_W))
            mix = even_mixer(q, k, v, u, vg, batch, seq, p["q_norm"][i], p["k_norm"][i],
                             p["attn_sink"][i], p["w_spatial"][i], p["b_spatial"][i], p["sgu_norm"][i])
        else:
            ga, gg, z = norm_proj(x2, p["norm_mix"][layer], lw["w_in"], (CONV_CH, CONV_CH, POOL_W))
            mix = odd_mixer(ga, gg, z, batch, seq, p["conv_w"][i], p["conv_b"][i], p["conv_norm"][i],
                            p["pool_w"][i], p["pool_scale"][i])
        x_new, h, aff = out_proj_router(x2, mix, lw["w_out"], p["norm_ffn"][layer],
                                        p["w_router"][layer], p["b_router"][layer])
        x2 = moe_layer(x_new, h, aff, lw["w_gate"], lw["w_up"], lw["w_down"])
    return x2.reshape(batch, seq, d)


def kernel(x_prompt, x_sample, norm_mix, norm_ffn, w_in_ab, w_out_ab, q_norm, k_norm, attn_sink, w_spatial, b_spatial, sgu_norm, w_in_cd, w_out_cd, conv_w, conv_b, conv_norm, pool_w, pool_scale, w_router, b_router, w_gate, w_up, w_down):
    p = dict(norm_mix=norm_mix, norm_ffn=norm_ffn, q_norm=q_norm, k_norm=k_norm, attn_sink=attn_sink,
             w_spatial=w_spatial, b_spatial=b_spatial, sgu_norm=sgu_norm, conv_w=conv_w, conv_b=conv_b,
             conv_norm=conv_norm, pool_w=pool_w, pool_scale=pool_scale, w_router=w_router,
             b_router=b_router)
    depth = norm_mix.shape[0]
    layer_weights = []
    for layer in range(depth):
        i = layer // 2
        w_in, w_out = (w_in_ab, w_out_ab) if layer % 2 == 0 else (w_in_cd, w_out_cd)
        layer_weights.append(dict(
            w_in=w_in[i].astype(BF16), w_out=w_out[i].astype(BF16),
            w_gate=w_gate[layer].astype(BF16), w_up=w_up[layer].astype(BF16),
            w_down=w_down[layer].astype(BF16)))
    y_prompt = _trunk(x_prompt, p, layer_weights)
    y_sample = _trunk(x_sample, p, layer_weights)
    return (y_prompt, y_sample)
```

```python
import functools

import jax
import jax.numpy as jnp
import numpy as np
from jax import lax
from jax.experimental import pallas as pl
from jax.experimental.pallas import tpu as pltpu

F32 = jnp.float32
BF16 = jnp.bfloat16

D_MODEL = 1024
HEAD_DIM = 64
ATTN_HEADS = 8
KV_HEADS = 2
WINDOW = 128
BLOCK = 128
ATTN_W = ATTN_HEADS * HEAD_DIM
KV_W = KV_HEADS * HEAD_DIM
SGU_GROUPS = 8
SGU_DIM = 64
SGU_W = SGU_GROUPS * SGU_DIM
CONV_CH = 512
CONV_WIDTH = 31
CONV_PAD = CONV_WIDTH // 2
POOL_GROUPS = 4
POOL_DIM = 128
POOL_W = POOL_GROUPS * POOL_DIM
POOL_WINDOWS = (2, 4, 8, 16)
N_EXPERTS = 16
EXPERT_FF = 1536
CAPACITY_FACTOR = 2
EPS = 1e-6
NEG_INF = -1e30
LANES = 128
HALO = 16

ROW_TILE = 512
MIX_TILE = 512
FFN_TILE = 512


def _dot(a, b):
    return jnp.dot(a, b, preferred_element_type=F32)


def _dot_nt(a, b):
    return lax.dot_general(a, b, (((1,), (1,)), ((), ())), preferred_element_type=F32)


def _norm_proj_kernel(x_ref, g_ref, w_ref, *out_refs, splits):
    x = x_ref[...]
    ms = jnp.mean(x * x, axis=-1, keepdims=True)
    h = (x * lax.rsqrt(ms + EPS) * g_ref[...]).astype(BF16)
    off = 0
    for o_ref, width in zip(out_refs, splits):
        o_ref[...] = _dot(h, w_ref[:, off:off + width]).astype(o_ref.dtype)
        off += width


def norm_proj(x2, gain, w_bf16, splits):
    n, d = x2.shape
    width = w_bf16.shape[1]
    tm = min(ROW_TILE, n)
    return pl.pallas_call(
        functools.partial(_norm_proj_kernel, splits=splits),
        grid=(n // tm,),
        in_specs=[
            pl.BlockSpec((tm, d), lambda i: (i, 0)),
            pl.BlockSpec((1, d), lambda i: (0, 0)),
            pl.BlockSpec((d, width), lambda i: (0, 0)),
        ],
        out_specs=[pl.BlockSpec((tm, s), lambda i: (i, 0)) for s in splits],
        out_shape=[jax.ShapeDtypeStruct((n, s), BF16) for s in splits],
        compiler_params=pltpu.CompilerParams(dimension_semantics=("parallel",)),
        name="norm_proj",
    )(x2, gain.reshape(1, d), w_bf16)


def _even_mixer_kernel(sink_ref, q_ref, kp_ref, k_ref, kn_ref, vp_ref, v_ref, vn_ref, u_ref, vg_ref,
                       bias_ref, qgain_ref, kgain_ref, ones_q_ref, ones_k_ref, w2_ref, bsp_ref,
                       sgn_ref, o_ref, *, n_tiles):
    i = pl.program_id(1)
    tq = q_ref.shape[0]
    nblk = tq // BLOCK
    lane = lax.broadcasted_iota(jnp.int32, (1, LANES), 1)
    lo_mask = (lane < HEAD_DIM).astype(F32)
    hi_mask = 1.0 - lo_mask

    qf = q_ref[...].astype(F32)
    q_ms = _dot((qf * qf).astype(BF16), ones_q_ref[...]) * (1.0 / HEAD_DIM)
    qn = (qf * lax.rsqrt(q_ms + EPS) * qgain_ref[...] * (HEAD_DIM ** -0.5)).astype(BF16)

    kf = jnp.concatenate([kp_ref[...], k_ref[...], kn_ref[...]], axis=0).astype(F32)
    k_ms = _dot((kf * kf).astype(BF16), ones_k_ref[...]) * (1.0 / HEAD_DIM)
    kn = kf * lax.rsqrt(k_ms + EPS) * kgain_ref[...]
    vf = jnp.concatenate([vp_ref[...], v_ref[...], vn_ref[...]], axis=0).astype(F32)
    kr = pltpu.roll(kn, HEAD_DIM, 1)
    vr = pltpu.roll(vf, HEAD_DIM, 1)
    k_lo = [(kn * lo_mask).astype(BF16), (kr * lo_mask).astype(BF16)]
    k_hi = [(kr * hi_mask).astype(BF16), (kn * hi_mask).astype(BF16)]
    v_lo = [(vf * lo_mask).astype(BF16), (vr * lo_mask).astype(BF16)]
    v_hi = [(vr * hi_mask).astype(BF16), (vf * hi_mask).astype(BF16)]

    col = lax.broadcasted_iota(jnp.int32, (1, 6 * BLOCK), 1)
    col_blk = (col // BLOCK) % 3
    row = lax.broadcasted_iota(jnp.int32, (2 * BLOCK, 1), 0)
    first_pair = row < BLOCK
    lane_lo = lane < HEAD_DIM

    uf = jax.nn.gelu(u_ref[...].astype(F32))
    gf = jax.nn.gelu(vg_ref[...].astype(F32))
    g_ms = _dot((gf * gf).astype(BF16), ones_q_ref[...]) * (1.0 / SGU_DIM)
    gn = gf * lax.rsqrt(g_ms + EPS) * sgn_ref[...]

    for j in range(nblk):
        r0 = j * BLOCK
        prev_ok = jnp.logical_or(i > 0, j > 0)
        next_ok = jnp.logical_or(i < n_tiles - 1, j < nblk - 1)
        col_ok = jnp.logical_and(jnp.logical_or(col_blk != 0, prev_ok),
                                 jnp.logical_or(col_blk != 2, next_ok))
        a_parts = []
        for g in range(KV_HEADS):
            kblk = jnp.concatenate([k_lo[g][r0:r0 + 3 * BLOCK], k_hi[g][r0:r0 + 3 * BLOCK]], axis=0)
            vblk = jnp.concatenate([v_lo[g][r0:r0 + 3 * BLOCK], v_hi[g][r0:r0 + 3 * BLOCK]], axis=0)
            c0 = g * 2 * LANES
            q2 = jnp.concatenate([qn[r0:r0 + BLOCK, c0:c0 + LANES],
                                  qn[r0:r0 + BLOCK, c0 + LANES:c0 + 2 * LANES]], axis=0)
            tab = bias_ref[g]
            s = _dot_nt(q2, kblk)
            ok = jnp.logical_and(tab > 0.5 * NEG_INF, col_ok)
            s = jnp.where(ok, s + tab, NEG_INF)
            sink_e = jnp.where(first_pair, sink_ref[4 * g], sink_ref[4 * g + 2])
            sink_o = jnp.where(first_pair, sink_ref[4 * g + 1], sink_ref[4 * g + 3])
            s_e = s[:, :3 * BLOCK]
            s_o = s[:, 3 * BLOCK:]
            m_e = jnp.maximum(jnp.max(s_e, axis=-1, keepdims=True), sink_e)
            m_o = jnp.maximum(jnp.max(s_o, axis=-1, keepdims=True), sink_o)
            p_e = jnp.exp(s_e - m_e)
            p_o = jnp.exp(s_o - m_o)
            d_e = jnp.sum(p_e, axis=-1, keepdims=True) + jnp.exp(sink_e - m_e)
            d_o = jnp.sum(p_o, axis=-1, keepdims=True) + jnp.exp(sink_o - m_o)
            p = jnp.concatenate([p_e, p_o], axis=1).astype(BF16)
            o = _dot(p, vblk)
            o = o * jnp.where(lane_lo, 1.0 / d_e, 1.0 / d_o)
            a_parts.append(o[:BLOCK])
            a_parts.append(o[BLOCK:])
        a_out = jnp.concatenate(a_parts, axis=1)

        gb = gn[r0:r0 + BLOCK]
        b_parts = []
        for pr in range(SGU_GROUPS // 2):
            gp = gb[:, pr * LANES:(pr + 1) * LANES]
            rhs = jnp.concatenate([(gp * lo_mask).astype(BF16), (gp * hi_mask).astype(BF16)], axis=0)
            b_parts.append(_dot(w2_ref[pr], rhs))
        mixed = jnp.concatenate(b_parts, axis=1) + bsp_ref[...]
        b_out = uf[r0:r0 + BLOCK] * mixed
        o_ref[r0:r0 + BLOCK, :] = jnp.concatenate([a_out, b_out], axis=1).astype(o_ref.dtype)


def _attn_bias_tables():
    slopes = np.exp2(-8.0 * np.arange(1, ATTN_HEADS + 1, dtype=np.float64) / ATTN_HEADS)
    t = np.arange(BLOCK)[:, None]
    c = np.arange(3 * BLOCK)[None, :]
    rel = np.abs(c - BLOCK - t)
    tabs = np.zeros((KV_HEADS, 2 * BLOCK, 6 * BLOCK), np.float32)
    for g in range(KV_HEADS):
        for pr in range(2):
            for half in range(2):
                h = 4 * g + 2 * pr + half
                tab = np.where(rel <= WINDOW, -slopes[h] * rel, NEG_INF)
                tabs[g, pr * BLOCK:(pr + 1) * BLOCK, half * 3 * BLOCK:(half + 1) * 3 * BLOCK] = tab
    return jnp.asarray(tabs)


def _block_ones(width, seg):
    idx = np.arange(width) // seg
    return jnp.asarray((idx[:, None] == idx[None, :]).astype(np.float32), dtype=BF16)


def even_mixer(q, k, v, u, vg, batch, seq, q_gain, k_gain, sink, w_spatial, b_spatial, sgu_norm):
    n = batch * seq
    tq = min(MIX_TILE, seq)
    n_tiles = seq // tq
    bpt = tq // BLOCK
    nb_total = n // BLOCK

    def main_map(b, i, *_):
        return (b * n_tiles + i, 0)

    def prev_map(b, i, *_):
        return (jnp.maximum((b * n_tiles + i) * bpt - 1, 0), 0)

    def next_map(b, i, *_):
        return (jnp.minimum((b * n_tiles + i + 1) * bpt, nb_total - 1), 0)

    def const2(b, i, *_):
        return (0, 0)

    def const3(b, i, *_):
        return (0, 0, 0)

    bias = _attn_bias_tables()
    ones_q = _block_ones(ATTN_W, HEAD_DIM)
    ones_k = _block_ones(KV_W, HEAD_DIM)
    qg = jnp.tile(q_gain, ATTN_HEADS).reshape(1, ATTN_W)
    kg = jnp.tile(k_gain, KV_HEADS).reshape(1, KV_W)
    w2 = w_spatial.reshape(SGU_GROUPS // 2, 2, BLOCK, BLOCK).transpose(0, 2, 1, 3)
    w2 = w2.reshape(SGU_GROUPS // 2, BLOCK, 2 * BLOCK).astype(BF16)
    bsp = jnp.repeat(b_spatial.T, SGU_DIM, axis=1)
    kernel = functools.partial(_even_mixer_kernel, n_tiles=n_tiles)
    grid_spec = pltpu.PrefetchScalarGridSpec(
        num_scalar_prefetch=1,
        grid=(batch, n_tiles),
        in_specs=[
            pl.BlockSpec((tq, ATTN_W), main_map),
            pl.BlockSpec((BLOCK, KV_W), prev_map),
            pl.BlockSpec((tq, KV_W), main_map),
            pl.BlockSpec((BLOCK, KV_W), next_map),
            pl.BlockSpec((BLOCK, KV_W), prev_map),
            pl.BlockSpec((tq, KV_W), main_map),
            pl.BlockSpec((BLOCK, KV_W), next_map),
            pl.BlockSpec((tq, SGU_W), main_map),
            pl.BlockSpec((tq, SGU_W), main_map),
            pl.BlockSpec((KV_HEADS, 2 * BLOCK, 6 * BLOCK), const3),
            pl.BlockSpec((1, ATTN_W), const2),
            pl.BlockSpec((1, KV_W), const2),
            pl.BlockSpec((ATTN_W, ATTN_W), const2),
            pl.BlockSpec((KV_W, KV_W), const2),
            pl.BlockSpec((SGU_GROUPS // 2, BLOCK, 2 * BLOCK), const3),
            pl.BlockSpec((BLOCK, SGU_W), const2),
            pl.BlockSpec((1, SGU_W), const2),
        ],
        out_specs=pl.BlockSpec((tq, D_MODEL), main_map),
    )
    return pl.pallas_call(
        kernel,
        grid_spec=grid_spec,
        out_shape=jax.ShapeDtypeStruct((n, D_MODEL), BF16),
        compiler_params=pltpu.CompilerParams(dimension_semantics=("parallel", "parallel")),
        name="even_mixer",
    )(sink.astype(F32), q, k, k, k, v, v, v, u, vg, bias, qg, kg, ones_q, ones_k, w2, bsp,
      sgu_norm.reshape(1, SGU_W))


def _odd_mixer_kernel(ap_ref, a_ref, an_ref, gp_ref, g_ref, gn_ref, zp_ref, z_ref, zn_ref,
                      cw_ref, cb_ref, cn_ref, pw_ref, ps_ref, o_ref, h_scr, z_scr, *, n_tiles, seq):
    i = pl.program_id(1)
    tq = a_ref.shape[0]
    prev_ok = (i > 0).astype(F32)
    next_ok = (i < n_tiles - 1).astype(F32)

    def glu(a, g):
        return a[...].astype(F32) * jax.nn.sigmoid(g[...].astype(F32))

    h_scr[0:HALO, :] = glu(ap_ref, gp_ref) * prev_ok
    h_scr[HALO:HALO + tq, :] = glu(a_ref, g_ref)
    h_scr[HALO + tq:, :] = glu(an_ref, gn_ref) * next_ok
    z_scr[0:HALO, :] = zp_ref[...].astype(F32) * prev_ok
    z_scr[HALO:HALO + tq, :] = z_ref[...].astype(F32)
    z_scr[HALO + tq:, :] = zn_ref[...].astype(F32) * next_ok

    rc = 64
    for c in range(tq // rc):
        r0 = c * rc
        acc = jnp.zeros((rc, CONV_CH), F32)
        for tap in range(CONV_WIDTH):
            start = r0 + HALO - CONV_PAD + tap
            acc = acc + cw_ref[tap:tap + 1, :] * h_scr[start:start + rc, :]
        acc = acc + cb_ref[...]
        ms = jnp.mean(acc * acc, axis=-1, keepdims=True)
        y = acc * lax.rsqrt(ms + EPS) * cn_ref[...]
        y = y * jax.nn.sigmoid(y)

        pos = (i * tq + r0 + lax.broadcasted_iota(jnp.int32, (rc, 1), 0))
        d_parts = []
        for g, w in enumerate(POOL_WINDOWS):
            l0 = g * POOL_DIM
            tot = jnp.zeros((rc, POOL_DIM), F32)
            for off in range(-(w // 2), w // 2):
                start = r0 + HALO + off
                tot = tot + z_scr[start:start + rc, l0:l0 + POOL_DIM]
            cnt = (jnp.minimum(pos + w // 2, seq) - jnp.maximum(pos - w // 2, 0)).astype(F32)
            zc = z_scr[r0 + HALO:r0 + HALO + rc, l0:l0 + POOL_DIM]
            pooled = (tot / cnt - zc).astype(BF16)
            d_parts.append(_dot(pooled, pw_ref[g]))
        d_out = jnp.concatenate(d_parts, axis=1) * ps_ref[...]
        o_ref[r0:r0 + rc, :] = jnp.concatenate([y, d_out], axis=1).astype(o_ref.dtype)


def odd_mixer(ga, gg, z, batch, seq, conv_w, conv_b, conv_norm, pool_w, pool_scale):
    n = batch * seq
    tq = min(MIX_TILE, seq)
    n_tiles = seq // tq
    hpt = tq // HALO
    nh_total = n // HALO

    def main_map(b, i):
        return (b * n_tiles + i, 0)

    def prev_map(b, i):
        return (jnp.maximum((b * n_tiles + i) * hpt - 1, 0), 0)

    def next_map(b, i):
        return (jnp.minimum((b * n_tiles + i + 1) * hpt, nh_total - 1), 0)

    def const2(b, i):
        return (0, 0)

    def const3(b, i):
        return (0, 0, 0)

    def halo_specs():
        return [pl.BlockSpec((HALO, CONV_CH), prev_map),
                pl.BlockSpec((tq, CONV_CH), main_map),
                pl.BlockSpec((HALO, CONV_CH), next_map)]

    kernel = functools.partial(_odd_mixer_kernel, n_tiles=n_tiles, seq=seq)
    return pl.pallas_call(
        kernel,
        grid=(batch, n_tiles),
        in_specs=halo_specs() + halo_specs() + halo_specs() + [
            pl.BlockSpec((CONV_WIDTH, CONV_CH), const2),
            pl.BlockSpec((1, CONV_CH), const2),
            pl.BlockSpec((1, CONV_CH), const2),
            pl.BlockSpec((POOL_GROUPS, POOL_DIM, POOL_DIM), const3),
            pl.BlockSpec((1, POOL_W), const2),
        ],
        out_specs=pl.BlockSpec((tq, D_MODEL), main_map),
        out_shape=jax.ShapeDtypeStruct((n, D_MODEL), BF16),
        scratch_shapes=[pltpu.VMEM((tq + 2 * HALO, CONV_CH), F32),
                        pltpu.VMEM((tq + 2 * HALO, POOL_W), F32)],
        compiler_params=pltpu.CompilerParams(dimension_semantics=("parallel", "parallel")),
        name="odd_mixer",
    )(ga, ga, ga, gg, gg, gg, z, z, z, conv_w, conv_b.reshape(1, CONV_CH),
      conv_norm.reshape(1, CONV_CH), pool_w.astype(BF16), pool_scale.reshape(1, POOL_W))


def _out_proj_kernel(x_ref, m_ref, w_ref, g_ref, wrt_ref, br_ref, xo_ref, h_ref, afft_ref):
    xn = x_ref[...] + _dot(m_ref[...], w_ref[...])
    xo_ref[...] = xn
    ms = jnp.mean(xn * xn, axis=-1, keepdims=True)
    h = (xn * lax.rsqrt(ms + EPS) * g_ref[...]).astype(BF16)
    h_ref[...] = h
    logits = _dot_nt(wrt_ref[...], h) + br_ref[...]
    m = jnp.max(logits, axis=0, keepdims=True)
    e = jnp.exp(logits - m)
    afft_ref[...] = e / jnp.sum(e, axis=0, keepdims=True)


def out_proj_router(x2, mix, w_out_bf16, norm_g, w_router, b_router):
    n, d = x2.shape
    tm = min(ROW_TILE, n)
    return pl.pallas_call(
        _out_proj_kernel,
        grid=(n // tm,),
        in_specs=[
            pl.BlockSpec((tm, d), lambda i: (i, 0)),
            pl.BlockSpec((tm, d), lambda i: (i, 0)),
            pl.BlockSpec((d, d), lambda i: (0, 0)),
            pl.BlockSpec((1, d), lambda i: (0, 0)),
            pl.BlockSpec((N_EXPERTS, d), lambda i: (0, 0)),
            pl.BlockSpec((N_EXPERTS, 1), lambda i: (0, 0)),
        ],
        out_specs=[
            pl.BlockSpec((tm, d), lambda i: (i, 0)),
            pl.BlockSpec((tm, d), lambda i: (i, 0)),
            pl.BlockSpec((N_EXPERTS, tm), lambda i: (0, i)),
        ],
        out_shape=[
            jax.ShapeDtypeStruct((n, d), F32),
            jax.ShapeDtypeStruct((n, d), BF16),
            jax.ShapeDtypeStruct((N_EXPERTS, n), F32),
        ],
        compiler_params=pltpu.CompilerParams(dimension_semantics=("parallel",)),
        name="out_proj_router",
    )(x2, mix, w_out_bf16, norm_g.reshape(1, d), w_router.T.astype(BF16),
      b_router.reshape(N_EXPERTS, 1))


def _expert_ffn_kernel(x_ref, wg_ref, wu_ref, wd_ref, gate_ref, o_ref):
    x = x_ref[...]
    g = _dot(x, wg_ref[...])
    u = _dot(x, wu_ref[...])
    hmid = (g * jax.nn.sigmoid(g) * u).astype(BF16)
    y = _dot(hmid, wd_ref[...])
    gate = gate_ref[...]
    o_ref[...] = y * jnp.concatenate([gate] * (D_MODEL // LANES), axis=1)


def expert_ffn(xe, w_gate, w_up, w_down, gates_b):
    e, cap, d = xe.shape
    ff = w_gate.shape[-1]
    tm = min(FFN_TILE, cap)
    return pl.pallas_call(
        _expert_ffn_kernel,
        grid=(e, cap // tm),
        in_specs=[
            pl.BlockSpec((None, tm, d), lambda ei, r: (ei, r, 0)),
            pl.BlockSpec((None, d, ff), lambda ei, r: (ei, 0, 0)),
            pl.BlockSpec((None, d, ff), lambda ei, r: (ei, 0, 0)),
            pl.BlockSpec((None, ff, d), lambda ei, r: (ei, 0, 0)),
            pl.BlockSpec((None, tm, LANES), lambda ei, r: (ei, r, 0)),
        ],
        out_specs=pl.BlockSpec((None, tm, d), lambda ei, r: (ei, r, 0)),
        out_shape=jax.ShapeDtypeStruct((e, cap, d), F32),
        compiler_params=pltpu.CompilerParams(
            dimension_semantics=("parallel", "arbitrary"),
            vmem_limit_bytes=56 * 1024 * 1024),
        name="expert_ffn",
    )(xe, w_gate, w_up, w_down, gates_b)


def _tri(shape, fn):
    r = lax.broadcasted_iota(jnp.int32, shape, 0)
    c = lax.broadcasted_iota(jnp.int32, shape, 1)
    return fn(r, c).astype(BF16)


def _route_kernel(aff_ref, gm_ref, carry_ref, gt_scr, eq_scr, need_scr, *, cap):
    ne, nb, _ = aff_ref.shape
    bits = pltpu.bitcast(aff_ref[...], jnp.int32)

    def count(flags):
        return jnp.sum(jnp.sum(flags.astype(F32), axis=1, keepdims=True), axis=2, keepdims=True)

    def bit_step(i, thr):
        cand = thr | (jnp.int32(1) << (30 - i))
        return jnp.where(count(bits >= cand) >= cap, cand, thr)

    thr = lax.fori_loop(0, 31, bit_step, jnp.zeros((ne, 1, 1), jnp.int32))
    gt = bits > thr
    gt_scr[...] = gt.astype(F32)
    eq_scr[...] = (bits == thr).astype(F32)
    need_scr[...] = jnp.broadcast_to(cap - count(gt), (ne, 1, LANES))

    u_incl = _tri((LANES, LANES), lambda r, c: r <= c)
    ones_l = jnp.ones((LANES, LANES), BF16)
    l_strict = _tri((nb, nb), lambda r, c: c < r)
    u_strict = _tri((nb, nb), lambda r, c: r < c)
    ones_8 = jnp.ones((8, LANES), BF16)

    def prefix(m):
        mb = m.astype(BF16)
        tot = _dot(mb, ones_l)
        carry = _dot(l_strict, tot.astype(BF16))
        return _dot(mb, u_incl) + carry - m, mb

    def per_expert(e, _):
        eq = eq_scr[e]
        eq_rank, _ = prefix(eq)
        mask = jnp.maximum(gt_scr[e], eq * (eq_rank < need_scr[e]).astype(F32))
        g_excl, mb = prefix(mask)
        gm_ref[e] = jnp.where(mask > 0.0, g_excl + 1.0, 0.5)
        tot_row = _dot_nt(ones_8, mb)
        carry_row = _dot(tot_row.astype(BF16), u_strict)[0:1]
        carry_ref[pl.ds(e, 1), :] = carry_row.astype(jnp.int32)
        return 0

    lax.fori_loop(0, ne, per_expert, 0)


def route(aff3, cap):
    ne, nb, _ = aff3.shape
    return pl.pallas_call(
        functools.partial(_route_kernel, cap=cap),
        out_shape=[jax.ShapeDtypeStruct((ne, nb, LANES), F32),
                   jax.ShapeDtypeStruct((ne, nb), jnp.int32)],
        scratch_shapes=[pltpu.VMEM((ne, nb, LANES), F32), pltpu.VMEM((ne, nb, LANES), F32),
                        pltpu.VMEM((ne, 1, LANES), F32)],
        compiler_params=pltpu.CompilerParams(vmem_limit_bytes=48 * 1024 * 1024),
        name="route",
    )(aff3)


HALF = 64


def _invert_kernel(carry_ref, gm_ref, aff_ref, idx_ref, gate_ref, *, nb, ct):
    e = pl.program_id(0)
    lane_f = lax.broadcasted_iota(jnp.int32, (1, LANES), 1).astype(F32)
    lane_i = lax.broadcasted_iota(jnp.int32, (HALF, LANES), 1)
    sub_i = lax.broadcasted_iota(jnp.int32, (HALF, LANES), 0)
    sub_f = lax.broadcasted_iota(jnp.int32, (HALF, 1), 0).astype(F32)

    def block_start(b):
        return carry_ref[e, jnp.minimum(b, nb - 1)]

    def half_tile(hh, state):
        ptr, row_i, row_g = state
        s0 = hh * HALF
        ptr = lax.while_loop(lambda q: jnp.logical_and(q + 1 < nb, block_start(q + 1) <= s0),
                             lambda q: q + 1, ptr)
        target = sub_f + jnp.asarray(s0 + 1, F32)

        def cond(st):
            return jnp.logical_and(st[0] < nb, block_start(st[0]) < s0 + HALF)

        def body(st):
            b, a_i, a_g = st
            hit = gm_ref[pl.ds(b, 1), :] == target
            tok = lane_f + jnp.asarray(b * LANES, F32)
            a_i = a_i + jnp.where(hit, tok, 0.0)
            a_g = a_g + jnp.where(hit, aff_ref[pl.ds(b, 1), :], 0.0)
            return b + 1, a_i, a_g

        z = jnp.zeros((HALF, LANES), F32)
        _, a_i, a_g = lax.while_loop(cond, body, (ptr, z, z))
        odd = hh % 2
        diag = sub_i + odd * HALF == lane_i

        def to_row(a):
            col = jnp.sum(a, axis=1, keepdims=True)
            return jnp.sum(jnp.where(diag, col, 0.0), axis=0, keepdims=True)

        row_i = row_i + to_row(a_i)
        row_g = row_g + to_row(a_g)

        @pl.when(odd == 1)
        def _():
            j = hh // 2
            idx_ref[pl.ds(j, 1), :] = row_i.astype(jnp.int32)
            gate_ref[pl.ds(j, 1), :] = row_g

        keep = jnp.asarray(1 - odd, F32)
        return ptr, row_i * keep, row_g * keep

    zr = jnp.zeros((1, LANES), F32)
    lax.fori_loop(0, 2 * ct, half_tile, (jnp.int32(0), zr, zr))


def invert(carry, gm, aff3, cap):
    ne, nb, _ = aff3.shape
    ct = cap // LANES
    kernel = functools.partial(_invert_kernel, nb=nb, ct=ct)
    tok_spec = pl.BlockSpec((None, nb, LANES), lambda e, *_: (e, 0, 0))
    slot_spec = pl.BlockSpec((None, ct, LANES), lambda e, *_: (e, 0, 0))
    grid_spec = pltpu.PrefetchScalarGridSpec(
        num_scalar_prefetch=1, grid=(ne,),
        in_specs=[tok_spec, tok_spec],
        out_specs=[slot_spec, slot_spec])
    return pl.pallas_call(
        kernel, grid_spec=grid_spec,
        out_shape=[jax.ShapeDtypeStruct((ne, ct, LANES), jnp.int32),
                   jax.ShapeDtypeStruct((ne, ct, LANES), F32)],
        compiler_params=pltpu.CompilerParams(dimension_semantics=("arbitrary",)),
        name="invert",
    )(carry, gm, aff3)


STRIP = 32
SUBLANES = 8
WIN = STRIP + SUBLANES


def _combine_kernel(carry_ref, x_ref, gmt_ref, ye_hbm, o_ref, buf, acc, sems, *, cap):
    b = pl.program_id(0)
    nbk = pl.num_programs(0)
    ne = N_EXPERTS
    slot = b % 2

    def window_start(t, e, r):
        first = carry_ref[e, t] + r * STRIP
        return jnp.minimum(first // SUBLANES * SUBLANES, cap - WIN)

    def strip_copy(t, e, r, s):
        start = pl.multiple_of(e * cap + window_start(t, e, r), SUBLANES)
        return pltpu.make_async_copy(ye_hbm.at[pl.ds(start, WIN), :],
                                     buf.at[s, pl.ds(e * WIN, WIN), :], sems.at[s])

    def start_round(t, r, s):
        for e in range(ne):
            strip_copy(t, e, r, s).start()

    def wait_round(s):
        pltpu.make_async_copy(ye_hbm.at[pl.ds(0, ne * WIN), :], buf.at[s], sems.at[s]).wait()

    @pl.when(b == 0)
    def _():
        start_round(0, 0, 0)

    @pl.when(b + 1 < nbk)
    def _():
        start_round(jnp.minimum(b + 1, nbk - 1), 0, 1 - slot)

    lane = lax.broadcasted_iota(jnp.int32, (1, LANES), 1)
    exp_r = lax.broadcasted_iota(jnp.int32, (LANES, ne * WIN), 0)
    exp_c = lax.broadcasted_iota(jnp.int32, (LANES, ne * WIN), 1)
    expand = (exp_c // WIN == exp_r).astype(BF16)
    col_in_win = (lax.broadcasted_iota(jnp.int32, (1, ne * WIN), 1) % WIN).astype(F32)
    slot_1b = gmt_ref[...]
    picked = slot_1b > 0.75
    acc[...] = jnp.zeros_like(acc)

    cmax = carry_ref[0, b + 1] - carry_ref[0, b]
    for e in range(1, ne):
        cmax = jnp.maximum(cmax, carry_ref[e, b + 1] - carry_ref[e, b])
    n_rounds = jnp.maximum((cmax + STRIP - 1) // STRIP, 1)

    def round_body(r, _):
        @pl.when(r > 0)
        def _():
            start_round(b, r, slot)

        wait_round(slot)
        want = jnp.zeros((1, LANES), F32)
        have = jnp.zeros((1, LANES), F32)
        for e in range(ne):
            first = carry_ref[e, b] + r * STRIP
            want = jnp.where(lane == e, jnp.asarray(first, F32), want)
            have = jnp.where(lane == e, jnp.asarray(window_start(b, e, r), F32), have)
        rel = slot_1b - 1.0 - want
        ok = jnp.logical_and(picked, jnp.logical_and(rel >= 0.0, rel < STRIP))
        pos = jnp.where(ok, slot_1b - 1.0 - have, -1.0).astype(BF16)
        onehot = (_dot(pos, expand) == col_in_win).astype(BF16)
        acc[...] += _dot(onehot, buf[slot].astype(BF16))
        return 0

    lax.fori_loop(0, n_rounds, round_body, 0)
    o_ref[...] = x_ref[...] + acc[...]


def combine(carry_ext, x_new, gmt, ye3, cap):
    n, d = x_new.shape
    nb = n // LANES
    grid_spec = pltpu.PrefetchScalarGridSpec(
        num_scalar_prefetch=1, grid=(nb,),
        in_specs=[pl.BlockSpec((LANES, d), lambda b, *_: (b, 0)),
                  pl.BlockSpec((LANES, LANES), lambda b, *_: (b, 0)),
                  pl.BlockSpec(memory_space=pl.ANY)],
        out_specs=pl.BlockSpec((LANES, d), lambda b, *_: (b, 0)),
        scratch_shapes=[pltpu.VMEM((2, N_EXPERTS * WIN, d), F32),
                        pltpu.VMEM((LANES, d), F32),
                        pltpu.SemaphoreType.DMA((2,))])
    return pl.pallas_call(
        functools.partial(_combine_kernel, cap=cap), grid_spec=grid_spec,
        out_shape=jax.ShapeDtypeStruct((n, d), F32),
        compiler_params=pltpu.CompilerParams(dimension_semantics=("arbitrary",)),
        name="combine",
    )(carry_ext, x_new, gmt, ye3)


def moe_layer(x_new, h, afft, w_gate, w_up, w_down):
    n, d = x_new.shape
    nb = n // LANES
    cap = max(1, CAPACITY_FACTOR * n // N_EXPERTS)
    aff3 = afft.reshape(N_EXPERTS, nb, LANES)
    gm, carry = route(aff3, cap)
    idx, gates = invert(carry, gm, aff3, cap)
    xe = h[idx.reshape(N_EXPERTS, cap)]
    gates_b = jnp.broadcast_to(gates.reshape(N_EXPERTS, cap, 1), (N_EXPERTS, cap, LANES))
    ye = expert_ffn(xe, w_gate, w_up, w_down, gates_b)
    gmt = jnp.pad(gm.reshape(N_EXPERTS, n).T, ((0, 0), (0, LANES - N_EXPERTS)), constant_values=0.5)
    carry_ext = jnp.concatenate([carry, jnp.full((N_EXPERTS, 1), cap, jnp.int32)], axis=1)
    return combine(carry_ext, x_new, gmt, ye.reshape(N_EXPERTS * cap, d), cap)


def _trunk(x, p, layer_weights):
    batch, seq, d = x.shape
    x2 = x.reshape(batch * seq, d)
    for layer, lw in enumerate(layer_weights):
        i = layer // 2
        if layer % 2 == 0:
            q, k, v, u, vg = norm_proj(x2, p["norm_mix"][layer], lw["w_in"],
                                       (ATTN_W, KV_W, KV_W, SGU_W, SGU_W))
            mix = even_mixer(q, k, v, u, vg, batch, seq, p["q_norm"][i], p["k_norm"][i],
                             p["attn_sink"][i], p["w_spatial"][i], p["b_spatial"][i], p["sgu_norm"][i])
        else:
            ga, gg, z = norm_proj(x2, p["norm_mix"][layer], lw["w_in"], (CONV_CH, CONV_CH, POOL_W))
            mix = odd_mixer(ga, gg, z, batch, seq, p["conv_w"][i], p["conv_b"][i], p["conv_norm"][i],
                            p["pool_w"][i], p["pool_scale"][i])
        x_new, h, aff = out_proj_router(x2, mix, lw["w_out"], p["norm_ffn"][layer],
                                        p["w_router"][layer], p["b_router"][layer])
        x2 = moe_layer(x_new, h, aff, lw["w_gate"], lw["w_up"], lw["w_down"])
    return x2.reshape(batch, seq, d)


def kernel(x_prompt, x_sample, norm_mix, norm_ffn, w_in_ab, w_out_ab, q_norm, k_norm, attn_sink, w_spatial, b_spatial, sgu_norm, w_in_cd, w_out_cd, conv_w, conv_b, conv_norm, pool_w, pool_scale, w_router, b_router, w_gate, w_up, w_down):
    p = dict(norm_mix=norm_mix, norm_ffn=norm_ffn, q_norm=q_norm, k_norm=k_norm, attn_sink=attn_sink,
             w_spatial=w_spatial, b_spatial=b_spatial, sgu_norm=sgu_norm, conv_w=conv_w, conv_b=conv_b,
             conv_norm=conv_norm, pool_w=pool_w, pool_scale=pool_scale, w_router=w_router,
             b_router=b_router)
    depth = norm_mix.shape[0]
    layer_weights = []
    for layer in range(depth):
        i = layer // 2
        w_in, w_out = (w_in_ab, w_out_ab) if layer % 2 == 0 else (w_in_cd, w_out_cd)
        layer_weights.append(dict(
            w_in=w_in[i].astype(BF16), w_out=w_out[i].astype(BF16),
            w_gate=w_gate[layer].astype(BF16), w_up=w_up[layer].astype(BF16),
            w_down=w_down[layer].astype(BF16)))
    y_prompt = _trunk(x_prompt, p, layer_weights)
    y_sample = _trunk(x_sample, p, layer_weights)
    return (y_prompt, y_sample)
```

```python
import functools

import jax
import jax.numpy as jnp
import numpy as np
from jax import lax
from jax.experimental import pallas as pl
from jax.experimental.pallas import tpu as pltpu

F32 = jnp.float32
BF16 = jnp.bfloat16

D_MODEL = 1024
HEAD_DIM = 64
ATTN_HEADS = 8
KV_HEADS = 2
WINDOW = 128
BLOCK = 128
ATTN_W = ATTN_HEADS * HEAD_DIM
KV_W = KV_HEADS * HEAD_DIM
SGU_GROUPS = 8
SGU_DIM = 64
SGU_W = SGU_GROUPS * SGU_DIM
CONV_CH = 512
CONV_WIDTH = 31
CONV_PAD = CONV_WIDTH // 2
POOL_GROUPS = 4
POOL_DIM = 128
POOL_W = POOL_GROUPS * POOL_DIM
POOL_WINDOWS = (2, 4, 8, 16)
N_EXPERTS = 16
EXPERT_FF = 1536
CAPACITY_FACTOR = 2
EPS = 1e-6
NEG_INF = -1e30
LANES = 128
HALO = 16
F32_SUBLANES = 8

ROW_TILE = 512
MIX_TILE = 512
FFN_TILE = 512


def _dot(a, b):
    return jnp.dot(a, b, preferred_element_type=F32)


def _dot_nt(a, b):
    return lax.dot_general(a, b, (((1,), (1,)), ((), ())), preferred_element_type=F32)


def _norm_proj_kernel(x_ref, g_ref, w_ref, *out_refs, splits):
    x = x_ref[...]
    ms = jnp.mean(x * x, axis=-1, keepdims=True)
    h = (x * lax.rsqrt(ms + EPS) * g_ref[...]).astype(BF16)
    off = 0
    for o_ref, width in zip(out_refs, splits):
        o_ref[...] = _dot(h, w_ref[:, off:off + width]).astype(o_ref.dtype)
        off += width


def norm_proj(x2, gain, w_bf16, splits):
    n, d = x2.shape
    width = w_bf16.shape[1]
    tm = min(ROW_TILE, n)
    return pl.pallas_call(
        functools.partial(_norm_proj_kernel, splits=splits),
        grid=(n // tm,),
        in_specs=[
            pl.BlockSpec((tm, d), lambda i: (i, 0)),
            pl.BlockSpec((1, d), lambda i: (0, 0)),
            pl.BlockSpec((d, width), lambda i: (0, 0)),
        ],
        out_specs=[pl.BlockSpec((tm, s), lambda i: (i, 0)) for s in splits],
        out_shape=[jax.ShapeDtypeStruct((n, s), BF16) for s in splits],
        compiler_params=pltpu.CompilerParams(dimension_semantics=("parallel",)),
        name="norm_proj",
    )(x2, gain.reshape(1, d), w_bf16)


def _even_mixer_kernel(sink_ref, q_ref, kp_ref, k_ref, kn_ref, vp_ref, v_ref, vn_ref, u_ref, vg_ref,
                       bias_ref, qgain_ref, kgain_ref, ones_q_ref, ones_k_ref, w2_ref, bsp_ref,
                       sgn_ref, o_ref, *, n_tiles):
    i = pl.program_id(1)
    tq = q_ref.shape[0]
    nblk = tq // BLOCK
    lane = lax.broadcasted_iota(jnp.int32, (1, LANES), 1)
    lo_mask = (lane < HEAD_DIM).astype(F32)
    hi_mask = 1.0 - lo_mask

    qf = q_ref[...].astype(F32)
    q_ms = _dot((qf * qf).astype(BF16), ones_q_ref[...]) * (1.0 / HEAD_DIM)
    qn = (qf * lax.rsqrt(q_ms + EPS) * qgain_ref[...] * (HEAD_DIM ** -0.5)).astype(BF16)

    kf = jnp.concatenate([kp_ref[...], k_ref[...], kn_ref[...]], axis=0).astype(F32)
    k_ms = _dot((kf * kf).astype(BF16), ones_k_ref[...]) * (1.0 / HEAD_DIM)
    kn = kf * lax.rsqrt(k_ms + EPS) * kgain_ref[...]
    vf = jnp.concatenate([vp_ref[...], v_ref[...], vn_ref[...]], axis=0).astype(F32)
    kr = pltpu.roll(kn, HEAD_DIM, 1)
    vr = pltpu.roll(vf, HEAD_DIM, 1)
    k_lo = [(kn * lo_mask).astype(BF16), (kr * lo_mask).astype(BF16)]
    k_hi = [(kr * hi_mask).astype(BF16), (kn * hi_mask).astype(BF16)]
    v_lo = [(vf * lo_mask).astype(BF16), (vr * lo_mask).astype(BF16)]
    v_hi = [(vr * hi_mask).astype(BF16), (vf * hi_mask).astype(BF16)]

    col = lax.broadcasted_iota(jnp.int32, (1, 6 * BLOCK), 1)
    col_blk = (col // BLOCK) % 3
    row = lax.broadcasted_iota(jnp.int32, (2 * BLOCK, 1), 0)
    first_pair = row < BLOCK
    lane_lo = lane < HEAD_DIM

    uf = jax.nn.gelu(u_ref[...].astype(F32))
    gf = jax.nn.gelu(vg_ref[...].astype(F32))
    g_ms = _dot((gf * gf).astype(BF16), ones_q_ref[...]) * (1.0 / SGU_DIM)
    gn = gf * lax.rsqrt(g_ms + EPS) * sgn_ref[...]

    for j in range(nblk):
        r0 = j * BLOCK
        prev_ok = jnp.logical_or(i > 0, j > 0)
        next_ok = jnp.logical_or(i < n_tiles - 1, j < nblk - 1)
        col_ok = jnp.logical_and(jnp.logical_or(col_blk != 0, prev_ok),
                                 jnp.logical_or(col_blk != 2, next_ok))
        a_parts = []
        for g in range(KV_HEADS):
            kblk = jnp.concatenate([k_lo[g][r0:r0 + 3 * BLOCK], k_hi[g][r0:r0 + 3 * BLOCK]], axis=0)
            vblk = jnp.concatenate([v_lo[g][r0:r0 + 3 * BLOCK], v_hi[g][r0:r0 + 3 * BLOCK]], axis=0)
            c0 = g * 2 * LANES
            q2 = jnp.concatenate([qn[r0:r0 + BLOCK, c0:c0 + LANES],
                                  qn[r0:r0 + BLOCK, c0 + LANES:c0 + 2 * LANES]], axis=0)
            tab = bias_ref[g]
            s = _dot_nt(q2, kblk)
            ok = jnp.logical_and(tab > 0.5 * NEG_INF, col_ok)
            s = jnp.where(ok, s + tab, NEG_INF)
            sink_e = jnp.where(first_pair, sink_ref[4 * g], sink_ref[4 * g + 2])
            sink_o = jnp.where(first_pair, sink_ref[4 * g + 1], sink_ref[4 * g + 3])
            s_e = s[:, :3 * BLOCK]
            s_o = s[:, 3 * BLOCK:]
            m_e = jnp.maximum(jnp.max(s_e, axis=-1, keepdims=True), sink_e)
            m_o = jnp.maximum(jnp.max(s_o, axis=-1, keepdims=True), sink_o)
            p_e = jnp.exp(s_e - m_e)
            p_o = jnp.exp(s_o - m_o)
            d_e = jnp.sum(p_e, axis=-1, keepdims=True) + jnp.exp(sink_e - m_e)
            d_o = jnp.sum(p_o, axis=-1, keepdims=True) + jnp.exp(sink_o - m_o)
            p = jnp.concatenate([p_e, p_o], axis=1).astype(BF16)
            o = _dot(p, vblk)
            o = o * jnp.where(lane_lo, 1.0 / d_e, 1.0 / d_o)
            a_parts.append(o[:BLOCK])
            a_parts.append(o[BLOCK:])
        a_out = jnp.concatenate(a_parts, axis=1)

        gb = gn[r0:r0 + BLOCK]
        b_parts = []
        for pr in range(SGU_GROUPS // 2):
            gp = gb[:, pr * LANES:(pr + 1) * LANES]
            rhs = jnp.concatenate([(gp * lo_mask).astype(BF16), (gp * hi_mask).astype(BF16)], axis=0)
            b_parts.append(_dot(w2_ref[pr], rhs))
        mixed = jnp.concatenate(b_parts, axis=1) + bsp_ref[...]
        b_out = uf[r0:r0 + BLOCK] * mixed
        o_ref[r0:r0 + BLOCK, :] = jnp.concatenate([a_out, b_out], axis=1).astype(o_ref.dtype)


def _attn_bias_tables():
    slopes = np.exp2(-8.0 * np.arange(1, ATTN_HEADS + 1, dtype=np.float64) / ATTN_HEADS)
    t = np.arange(BLOCK)[:, None]
    c = np.arange(3 * BLOCK)[None, :]
    rel = np.abs(c - BLOCK - t)
    tabs = np.zeros((KV_HEADS, 2 * BLOCK, 6 * BLOCK), np.float32)
    for g in range(KV_HEADS):
        for pr in range(2):
            for half in range(2):
                h = 4 * g + 2 * pr + half
                tab = np.where(rel <= WINDOW, -slopes[h] * rel, NEG_INF)
                tabs[g, pr * BLOCK:(pr + 1) * BLOCK, half * 3 * BLOCK:(half + 1) * 3 * BLOCK] = tab
    return jnp.asarray(tabs)


def _block_ones(width, seg):
    idx = np.arange(width) // seg
    return jnp.asarray((idx[:, None] == idx[None, :]).astype(np.float32), dtype=BF16)


def even_mixer(q, k, v, u, vg, batch, seq, q_gain, k_gain, sink, w_spatial, b_spatial, sgu_norm):
    n = batch * seq
    tq = min(MIX_TILE, seq)
    n_tiles = seq // tq
    bpt = tq // BLOCK
    nb_total = n // BLOCK

    def main_map(b, i, *_):
        return (b * n_tiles + i, 0)

    def prev_map(b, i, *_):
        return (jnp.maximum((b * n_tiles + i) * bpt - 1, 0), 0)

    def next_map(b, i, *_):
        return (jnp.minimum((b * n_tiles + i + 1) * bpt, nb_total - 1), 0)

    def const2(b, i, *_):
        return (0, 0)

    def const3(b, i, *_):
        return (0, 0, 0)

    bias = _attn_bias_tables()
    ones_q = _block_ones(ATTN_W, HEAD_DIM)
    ones_k = _block_ones(KV_W, HEAD_DIM)
    qg = jnp.tile(q_gain, ATTN_HEADS).reshape(1, ATTN_W)
    kg = jnp.tile(k_gain, KV_HEADS).reshape(1, KV_W)
    w2 = w_spatial.reshape(SGU_GROUPS // 2, 2, BLOCK, BLOCK).transpose(0, 2, 1, 3)
    w2 = w2.reshape(SGU_GROUPS // 2, BLOCK, 2 * BLOCK).astype(BF16)
    bsp = jnp.repeat(b_spatial.T, SGU_DIM, axis=1)
    kernel = functools.partial(_even_mixer_kernel, n_tiles=n_tiles)
    grid_spec = pltpu.PrefetchScalarGridSpec(
        num_scalar_prefetch=1,
        grid=(batch, n_tiles),
        in_specs=[
            pl.BlockSpec((tq, ATTN_W), main_map),
            pl.BlockSpec((BLOCK, KV_W), prev_map),
            pl.BlockSpec((tq, KV_W), main_map),
            pl.BlockSpec((BLOCK, KV_W), next_map),
            pl.BlockSpec((BLOCK, KV_W), prev_map),
            pl.BlockSpec((tq, KV_W), main_map),
            pl.BlockSpec((BLOCK, KV_W), next_map),
            pl.BlockSpec((tq, SGU_W), main_map),
            pl.BlockSpec((tq, SGU_W), main_map),
            pl.BlockSpec((KV_HEADS, 2 * BLOCK, 6 * BLOCK), const3),
            pl.BlockSpec((1, ATTN_W), const2),
            pl.BlockSpec((1, KV_W), const2),
            pl.BlockSpec((ATTN_W, ATTN_W), const2),
            pl.BlockSpec((KV_W, KV_W), const2),
            pl.BlockSpec((SGU_GROUPS // 2, BLOCK, 2 * BLOCK), const3),
            pl.BlockSpec((BLOCK, SGU_W), const2),
            pl.BlockSpec((1, SGU_W), const2),
        ],
        out_specs=pl.BlockSpec((tq, D_MODEL), main_map),
    )
    return pl.pallas_call(
        kernel,
        grid_spec=grid_spec,
        out_shape=jax.ShapeDtypeStruct((n, D_MODEL), BF16),
        compiler_params=pltpu.CompilerParams(dimension_semantics=("parallel", "parallel")),
        name="even_mixer",
    )(sink.astype(F32), q, k, k, k, v, v, v, u, vg, bias, qg, kg, ones_q, ones_k, w2, bsp,
      sgu_norm.reshape(1, SGU_W))


def _odd_mixer_kernel(ap_ref, a_ref, an_ref, gp_ref, g_ref, gn_ref, zp_ref, z_ref, zn_ref,
                      cw_ref, cb_ref, cn_ref, pw_ref, ps_ref, o_ref, h_scr, z_scr, hs_scr, *, n_tiles, seq):
    i = pl.program_id(1)
    tq = a_ref.shape[0]
    prev_ok = (i > 0).astype(F32)
    next_ok = (i < n_tiles - 1).astype(F32)

    def glu(a, g):
        return a[...].astype(F32) * jax.nn.sigmoid(g[...].astype(F32))

    h_scr[0:HALO, :] = glu(ap_ref, gp_ref) * prev_ok
    h_scr[HALO:HALO + tq, :] = glu(a_ref, g_ref)
    h_scr[HALO + tq:, :] = glu(an_ref, gn_ref) * next_ok
    z_scr[0:HALO, :] = zp_ref[...].astype(F32) * prev_ok
    z_scr[HALO:HALO + tq, :] = z_ref[...].astype(F32)
    z_scr[HALO + tq:, :] = zn_ref[...].astype(F32) * next_ok

    rc = 64
    shifted_rows = tq + 2 * HALO - F32_SUBLANES
    for p in range(1, F32_SUBLANES):
        for c0 in range(0, shifted_rows, rc):
            rows = min(rc, shifted_rows - c0)
            hs_scr[p, c0:c0 + rows, :] = h_scr[c0 + p:c0 + p + rows, :]

    for c in range(tq // rc):
        r0 = c * rc
        acc = jnp.zeros((rc, CONV_CH), F32)
        for tap in range(CONV_WIDTH):
            off = HALO - CONV_PAD + tap
            phase = off % F32_SUBLANES
            start = r0 + off - phase
            if phase == 0:
                window = h_scr[start:start + rc, :]
            else:
                window = hs_scr[phase, start:start + rc, :]
            acc = acc + cw_ref[tap:tap + 1, :] * window
        acc = acc + cb_ref[...]
        ms = jnp.mean(acc * acc, axis=-1, keepdims=True)
        y = acc * lax.rsqrt(ms + EPS) * cn_ref[...]
        y = y * jax.nn.sigmoid(y)

        pos = (i * tq + r0 + lax.broadcasted_iota(jnp.int32, (rc, 1), 0))
        d_parts = []
        for g, w in enumerate(POOL_WINDOWS):
            l0 = g * POOL_DIM
            tot = jnp.zeros((rc, POOL_DIM), F32)
            for off in range(-(w // 2), w // 2):
                start = r0 + HALO + off
                tot = tot + z_scr[start:start + rc, l0:l0 + POOL_DIM]
            cnt = (jnp.minimum(pos + w // 2, seq) - jnp.maximum(pos - w // 2, 0)).astype(F32)
            zc = z_scr[r0 + HALO:r0 + HALO + rc, l0:l0 + POOL_DIM]
            pooled = (tot / cnt - zc).astype(BF16)
            d_parts.append(_dot(pooled, pw_ref[g]))
        d_out = jnp.concatenate(d_parts, axis=1) * ps_ref[...]
        o_ref[r0:r0 + rc, :] = jnp.concatenate([y, d_out], axis=1).astype(o_ref.dtype)


def odd_mixer(ga, gg, z, batch, seq, conv_w, conv_b, conv_norm, pool_w, pool_scale):
    n = batch * seq
    tq = min(MIX_TILE, seq)
    n_tiles = seq // tq
    hpt = tq // HALO
    nh_total = n // HALO

    def main_map(b, i):
        return (b * n_tiles + i, 0)

    def prev_map(b, i):
        return (jnp.maximum((b * n_tiles + i) * hpt - 1, 0), 0)

    def next_map(b, i):
        return (jnp.minimum((b * n_tiles + i + 1) * hpt, nh_total - 1), 0)

    def const2(b, i):
        return (0, 0)

    def const3(b, i):
        return (0, 0, 0)

    def halo_specs():
        return [pl.BlockSpec((HALO, CONV_CH), prev_map),
                pl.BlockSpec((tq, CONV_CH), main_map),
                pl.BlockSpec((HALO, CONV_CH), next_map)]

    kernel = functools.partial(_odd_mixer_kernel, n_tiles=n_tiles, seq=seq)
    return pl.pallas_call(
        kernel,
        grid=(batch, n_tiles),
        in_specs=halo_specs() + halo_specs() + halo_specs() + [
            pl.BlockSpec((CONV_WIDTH, CONV_CH), const2),
            pl.BlockSpec((1, CONV_CH), const2),
            pl.BlockSpec((1, CONV_CH), const2),
            pl.BlockSpec((POOL_GROUPS, POOL_DIM, POOL_DIM), const3),
            pl.BlockSpec((1, POOL_W), const2),
        ],
        out_specs=pl.BlockSpec((tq, D_MODEL), main_map),
        out_shape=jax.ShapeDtypeStruct((n, D_MODEL), BF16),
        scratch_shapes=[pltpu.VMEM((tq + 2 * HALO, CONV_CH), F32),
                        pltpu.VMEM((tq + 2 * HALO, POOL_W), F32),
                        pltpu.VMEM((F32_SUBLANES, tq + 2 * HALO, CONV_CH), F32)],
        compiler_params=pltpu.CompilerParams(dimension_semantics=("parallel", "parallel")),
        name="odd_mixer",
    )(ga, ga, ga, gg, gg, gg, z, z, z, conv_w, conv_b.reshape(1, CONV_CH),
      conv_norm.reshape(1, CONV_CH), pool_w.astype(BF16), pool_scale.reshape(1, POOL_W))


def _out_proj_kernel(x_ref, m_ref, w_ref, g_ref, wrt_ref, br_ref, xo_ref, h_ref, afft_ref):
    xn = x_ref[...] + _dot(m_ref[...], w_ref[...])
    xo_ref[...] = xn
    ms = jnp.mean(xn * xn, axis=-1, keepdims=True)
    h = (xn * lax.rsqrt(ms + EPS) * g_ref[...]).astype(BF16)
    h_ref[...] = h
    logits = _dot_nt(wrt_ref[...], h) + br_ref[...]
    m = jnp.max(logits, axis=0, keepdims=True)
    e = jnp.exp(logits - m)
    afft_ref[...] = e / jnp.sum(e, axis=0, keepdims=True)


def out_proj_router(x2, mix, w_out_bf16, norm_g, w_router, b_router):
    n, d = x2.shape
    tm = min(ROW_TILE, n)
    return pl.pallas_call(
        _out_proj_kernel,
        grid=(n // tm,),
        in_specs=[
            pl.BlockSpec((tm, d), lambda i: (i, 0)),
            pl.BlockSpec((tm, d), lambda i: (i, 0)),
            pl.BlockSpec((d, d), lambda i: (0, 0)),
            pl.BlockSpec((1, d), lambda i: (0, 0)),
            pl.BlockSpec((N_EXPERTS, d), lambda i: (0, 0)),
            pl.BlockSpec((N_EXPERTS, 1), lambda i: (0, 0)),
        ],
        out_specs=[
            pl.BlockSpec((tm, d), lambda i: (i, 0)),
            pl.BlockSpec((tm, d), lambda i: (i, 0)),
            pl.BlockSpec((N_EXPERTS, tm), lambda i: (0, i)),
        ],
        out_shape=[
            jax.ShapeDtypeStruct((n, d), F32),
            jax.ShapeDtypeStruct((n, d), BF16),
            jax.ShapeDtypeStruct((N_EXPERTS, n), F32),
        ],
        compiler_params=pltpu.CompilerParams(dimension_semantics=("parallel",)),
        name="out_proj_router",
    )(x2, mix, w_out_bf16, norm_g.reshape(1, d), w_router.T.astype(BF16),
      b_router.reshape(N_EXPERTS, 1))


def _expert_ffn_kernel(x_ref, wg_ref, wu_ref, wd_ref, gate_ref, o_ref):
    x = x_ref[...]
    g = _dot(x, wg_ref[...])
    u = _dot(x, wu_ref[...])
    hmid = (g * jax.nn.sigmoid(g) * u).astype(BF16)
    y = _dot(hmid, wd_ref[...])
    gate = gate_ref[...]
    o_ref[...] = (y * jnp.concatenate([gate] * (D_MODEL // LANES), axis=1)).astype(o_ref.dtype)


def expert_ffn(xe, w_gate, w_up, w_down, gates_b, layer):
    e, cap, d = xe.shape
    ff = w_gate.shape[-1]
    tm = min(FFN_TILE, cap)
    return pl.pallas_call(
        _expert_ffn_kernel,
        grid=(e, cap // tm),
        in_specs=[
            pl.BlockSpec((None, tm, d), lambda ei, r: (ei, r, 0)),
            pl.BlockSpec((None, None, d, ff), lambda ei, r: (layer, ei, 0, 0)),
            pl.BlockSpec((None, None, d, ff), lambda ei, r: (layer, ei, 0, 0)),
            pl.BlockSpec((None, None, ff, d), lambda ei, r: (layer, ei, 0, 0)),
            pl.BlockSpec((None, tm, LANES), lambda ei, r: (ei, r, 0)),
        ],
        out_specs=pl.BlockSpec((None, tm, d), lambda ei, r: (ei, r, 0)),
        out_shape=jax.ShapeDtypeStruct((e, cap, d), BF16),
        compiler_params=pltpu.CompilerParams(
            dimension_semantics=("parallel", "arbitrary"),
            vmem_limit_bytes=56 * 1024 * 1024),
        name="expert_ffn",
    )(xe, w_gate, w_up, w_down, gates_b)


def _tri(shape, fn):
    r = lax.broadcasted_iota(jnp.int32, shape, 0)
    c = lax.broadcasted_iota(jnp.int32, shape, 1)
    return fn(r, c).astype(BF16)


def _route_kernel(aff_ref, gm_ref, carry_ref, gt_scr, eq_scr, need_scr, *, cap):
    ne, nb, _ = aff_ref.shape
    bits = pltpu.bitcast(aff_ref[...], jnp.int32)

    def count(flags):
        return jnp.sum(jnp.sum(flags.astype(F32), axis=1, keepdims=True), axis=2, keepdims=True)

    def bit_step(i, thr):
        cand = thr | (jnp.int32(1) << (30 - i))
        return jnp.where(count(bits >= cand) >= cap, cand, thr)

    thr = lax.fori_loop(0, 31, bit_step, jnp.zeros((ne, 1, 1), jnp.int32))
    gt = bits > thr
    gt_scr[...] = gt.astype(F32)
    eq_scr[...] = (bits == thr).astype(F32)
    need_scr[...] = jnp.broadcast_to(cap - count(gt), (ne, 1, LANES))

    u_incl = _tri((LANES, LANES), lambda r, c: r <= c)
    ones_l = jnp.ones((LANES, LANES), BF16)
    l_strict = _tri((nb, nb), lambda r, c: c < r)
    u_strict = _tri((nb, nb), lambda r, c: r < c)
    ones_8 = jnp.ones((8, LANES), BF16)

    def prefix(m):
        mb = m.astype(BF16)
        tot = _dot(mb, ones_l)
        carry = _dot(l_strict, tot.astype(BF16))
        return _dot(mb, u_incl) + carry - m, mb

    def per_expert(e, _):
        eq = eq_scr[e]
        eq_rank, _ = prefix(eq)
        mask = jnp.maximum(gt_scr[e], eq * (eq_rank < need_scr[e]).astype(F32))
        g_excl, mb = prefix(mask)
        gm_ref[e] = jnp.where(mask > 0.0, g_excl + 1.0, 0.5)
        tot_row = _dot_nt(ones_8, mb)
        carry_row = _dot(tot_row.astype(BF16), u_strict)[0:1]
        carry_ref[pl.ds(e, 1), :] = carry_row.astype(jnp.int32)
        return 0

    lax.fori_loop(0, ne, per_expert, 0)


def route(aff3, cap):
    ne, nb, _ = aff3.shape
    return pl.pallas_call(
        functools.partial(_route_kernel, cap=cap),
        out_shape=[jax.ShapeDtypeStruct((ne, nb, LANES), F32),
                   jax.ShapeDtypeStruct((ne, nb), jnp.int32)],
        scratch_shapes=[pltpu.VMEM((ne, nb, LANES), F32), pltpu.VMEM((ne, nb, LANES), F32),
                        pltpu.VMEM((ne, 1, LANES), F32)],
        compiler_params=pltpu.CompilerParams(vmem_limit_bytes=48 * 1024 * 1024),
        name="route",
    )(aff3)


def _invert_kernel(carry_ref, gm_ref, aff_ref, idx_ref, gate_ref, *, nb, ct):
    e = pl.program_id(0)
    lane_f = lax.broadcasted_iota(jnp.int32, (1, LANES), 1).astype(F32)
    lane_i = lax.broadcasted_iota(jnp.int32, (LANES, LANES), 1)
    sub_i = lax.broadcasted_iota(jnp.int32, (LANES, LANES), 0)
    diag = sub_i == lane_i
    sub_f = lax.broadcasted_iota(jnp.int32, (LANES, 1), 0).astype(F32)

    def block_start(b):
        return carry_ref[e, jnp.minimum(b, nb - 1)]

    def slot_tile(j, ptr):
        s0 = j * LANES
        ptr = lax.while_loop(lambda q: jnp.logical_and(q + 1 < nb, block_start(q + 1) <= s0),
                             lambda q: q + 1, ptr)
        target = sub_f + jnp.asarray(s0 + 1, F32)

        def cond(st):
            return jnp.logical_and(st[0] < nb, block_start(st[0]) < s0 + LANES)

        def body(st):
            b, a_i, a_g = st
            hit = gm_ref[pl.ds(b, 1), :] == target
            tok = lane_f + jnp.asarray(b * LANES, F32)
            a_i = a_i + jnp.where(hit, tok, 0.0)
            a_g = a_g + jnp.where(hit, aff_ref[pl.ds(b, 1), :], 0.0)
            return b + 1, a_i, a_g

        z = jnp.zeros((LANES, LANES), F32)
        _, a_i, a_g = lax.while_loop(cond, body, (ptr, z, z))

        def to_row(a):
            col = jnp.sum(a, axis=1, keepdims=True)
            return jnp.sum(jnp.where(diag, col, 0.0), axis=0, keepdims=True)

        idx_ref[pl.ds(j, 1), :] = to_row(a_i).astype(jnp.int32)
        gate_ref[pl.ds(j, 1), :] = to_row(a_g)
        return ptr

    lax.fori_loop(0, ct, slot_tile, jnp.int32(0))


def invert(carry, gm, aff3, cap):
    ne, nb, _ = aff3.shape
    ct = cap // LANES
    kernel = functools.partial(_invert_kernel, nb=nb, ct=ct)
    tok_spec = pl.BlockSpec((None, nb, LANES), lambda e, *_: (e, 0, 0))
    slot_spec = pl.BlockSpec((None, ct, LANES), lambda e, *_: (e, 0, 0))
    grid_spec = pltpu.PrefetchScalarGridSpec(
        num_scalar_prefetch=1, grid=(ne,),
        in_specs=[tok_spec, tok_spec],
        out_specs=[slot_spec, slot_spec])
    return pl.pallas_call(
        kernel, grid_spec=grid_spec,
        out_shape=[jax.ShapeDtypeStruct((ne, ct, LANES), jnp.int32),
                   jax.ShapeDtypeStruct((ne, ct, LANES), F32)],
        compiler_params=pltpu.CompilerParams(dimension_semantics=("arbitrary",)),
        name="invert",
    )(carry, gm, aff3)


STRIP = 32
SUBLANES = 16
WIN = STRIP + SUBLANES


def _combine_kernel(carry_ref, x_ref, gmt_ref, ye_hbm, o_ref, buf, acc, sems, *, cap):
    b = pl.program_id(0)
    nbk = pl.num_programs(0)
    ne = N_EXPERTS
    slot = b % 2

    def window_start(t, e, r):
        first = carry_ref[e, t] + r * STRIP
        return jnp.minimum(first // SUBLANES * SUBLANES, cap - WIN)

    def strip_copy(t, e, r, s):
        start = pl.multiple_of(e * cap + window_start(t, e, r), SUBLANES)
        return pltpu.make_async_copy(ye_hbm.at[pl.ds(start, WIN), :],
                                     buf.at[s, pl.ds(e * WIN, WIN), :], sems.at[s])

    def start_round(t, r, s):
        for e in range(ne):
            strip_copy(t, e, r, s).start()

    def wait_round(s):
        pltpu.make_async_copy(ye_hbm.at[pl.ds(0, ne * WIN), :], buf.at[s], sems.at[s]).wait()

    @pl.when(b == 0)
    def _():
        start_round(0, 0, 0)

    @pl.when(b + 1 < nbk)
    def _():
        start_round(jnp.minimum(b + 1, nbk - 1), 0, 1 - slot)

    lane = lax.broadcasted_iota(jnp.int32, (1, LANES), 1)
    exp_r = lax.broadcasted_iota(jnp.int32, (LANES, ne * WIN), 0)
    exp_c = lax.broadcasted_iota(jnp.int32, (LANES, ne * WIN), 1)
    expand = (exp_c // WIN == exp_r).astype(BF16)
    col_in_win = (lax.broadcasted_iota(jnp.int32, (1, ne * WIN), 1) % WIN).astype(F32)
    slot_1b = gmt_ref[...]
    picked = slot_1b > 0.75
    acc[...] = jnp.zeros_like(acc)

    cmax = carry_ref[0, b + 1] - carry_ref[0, b]
    for e in range(1, ne):
        cmax = jnp.maximum(cmax, carry_ref[e, b + 1] - carry_ref[e, b])
    n_rounds = jnp.maximum((cmax + STRIP - 1) // STRIP, 1)

    def round_body(r, _):
        @pl.when(r > 0)
        def _():
            start_round(b, r, slot)

        wait_round(slot)
        want = jnp.zeros((1, LANES), F32)
        have = jnp.zeros((1, LANES), F32)
        for e in range(ne):
            first = carry_ref[e, b] + r * STRIP
            want = jnp.where(lane == e, jnp.asarray(first, F32), want)
            have = jnp.where(lane == e, jnp.asarray(window_start(b, e, r), F32), have)
        rel = slot_1b - 1.0 - want
        ok = jnp.logical_and(picked, jnp.logical_and(rel >= 0.0, rel < STRIP))
        pos = jnp.where(ok, slot_1b - 1.0 - have, -1.0).astype(BF16)
        onehot = (_dot(pos, expand) == col_in_win).astype(BF16)
        acc[...] += _dot(onehot, buf[slot])
        return 0

    lax.fori_loop(0, n_rounds, round_body, 0)
    o_ref[...] = x_ref[...] + acc[...]


def combine(carry_ext, x_new, gmt, ye3, cap):
    n, d = x_new.shape
    nb = n // LANES
    grid_spec = pltpu.PrefetchScalarGridSpec(
        num_scalar_prefetch=1, grid=(nb,),
        in_specs=[pl.BlockSpec((LANES, d), lambda b, *_: (b, 0)),
                  pl.BlockSpec((LANES, LANES), lambda b, *_: (b, 0)),
                  pl.BlockSpec(memory_space=pl.ANY)],
        out_specs=pl.BlockSpec((LANES, d), lambda b, *_: (b, 0)),
        scratch_shapes=[pltpu.VMEM((2, N_EXPERTS * WIN, d), BF16),
                        pltpu.VMEM((LANES, d), F32),
                        pltpu.SemaphoreType.DMA((2,))])
    return pl.pallas_call(
        functools.partial(_combine_kernel, cap=cap), grid_spec=grid_spec,
        out_shape=jax.ShapeDtypeStruct((n, d), F32),
        compiler_params=pltpu.CompilerParams(dimension_semantics=("arbitrary",)),
        name="combine",
    )(carry_ext, x_new, gmt, ye3)


def moe_layer(x_new, h, afft, w_gate, w_up, w_down, layer):
    n, d = x_new.shape
    nb = n // LANES
    cap = max(1, CAPACITY_FACTOR * n // N_EXPERTS)
    aff3 = afft.reshape(N_EXPERTS, nb, LANES)
    gm, carry = route(aff3, cap)
    idx, gates = invert(carry, gm, aff3, cap)
    xe = h[idx.reshape(N_EXPERTS, cap)]
    gates_b = jnp.broadcast_to(gates.reshape(N_EXPERTS, cap, 1), (N_EXPERTS, cap, LANES))
    ye = expert_ffn(xe, w_gate, w_up, w_down, gates_b, layer)
    gmt = jnp.pad(gm.reshape(N_EXPERTS, n).T, ((0, 0), (0, LANES - N_EXPERTS)), constant_values=0.5)
    carry_ext = jnp.concatenate([carry, jnp.full((N_EXPERTS, 1), cap, jnp.int32)], axis=1)
    return combine(carry_ext, x_new, gmt, ye.reshape(N_EXPERTS * cap, d), cap)


def _trunk(x, p, layer_weights):
    batch, seq, d = x.shape
    x2 = x.reshape(batch * seq, d)
    for layer, lw in enumerate(layer_weights):
        i = layer // 2
        if layer % 2 == 0:
            q, k, v, u, vg = norm_proj(x2, p["norm_mix"][layer], lw["w_in"],
                                       (ATTN_W, KV_W, KV_W, SGU_W, SGU_W))
            mix = even_mixer(q, k, v, u, vg, batch, seq, p["q_norm"][i], p["k_norm"][i],
                             p["attn_sink"][i], p["w_spatial"][i], p["b_spatial"][i], p["sgu_norm"][i])
        else:
            ga, gg, z = norm_proj(x2, p["norm_mix"][layer], lw["w_in"], (CONV_CH, CONV_CH, POOL_W))
            mix = odd_mixer(ga, gg, z, batch, seq, p["conv_w"][i], p["conv_b"][i], p["conv_norm"][i],
                            p["pool_w"][i], p["pool_scale"][i])
        x_new, h, aff = out_proj_router(x2, mix, lw["w_out"], p["norm_ffn"][layer],
                                        p["w_router"][layer], p["b_router"][layer])
        x2 = moe_layer(x_new, h, aff, p["w_gate"], p["w_up"], p["w_down"], layer)
    return x2.reshape(batch, seq, d)


def kernel(x_prompt, x_sample, norm_mix, norm_ffn, w_in_ab, w_out_ab, q_norm, k_norm, attn_sink, w_spatial, b_spatial, sgu_norm, w_in_cd, w_out_cd, conv_w, conv_b, conv_norm, pool_w, pool_scale, w_router, b_router, w_gate, w_up, w_down):
    p = dict(norm_mix=norm_mix, norm_ffn=norm_ffn, q_norm=q_norm, k_norm=k_norm, attn_sink=attn_sink,
             w_spatial=w_spatial, b_spatial=b_spatial, sgu_norm=sgu_norm, conv_w=conv_w, conv_b=conv_b,
             conv_norm=conv_norm, pool_w=pool_w, pool_scale=pool_scale, w_router=w_router,
             b_router=b_router)
    depth = norm_mix.shape[0]
    layer_weights = []
    for layer in range(depth):
        i = layer // 2
        w_in, w_out = (w_in_ab, w_out_ab) if layer % 2 == 0 else (w_in_cd, w_out_cd)
        layer_weights.append(dict(w_in=w_in[i].astype(BF16), w_out=w_out[i].astype(BF16)))
    p.update(w_gate=w_gate.astype(BF16), w_up=w_up.astype(BF16), w_down=w_down.astype(BF16))
    y_prompt = _trunk(x_prompt, p, layer_weights)
    y_sample = _trunk(x_sample, p, layer_weights)
    return (y_prompt, y_sample)
```
